```python
import functools
import jax, jax.numpy as jnp
from jax import lax
import numpy as np

D_MODEL = 2048
BATCH = 8
SEQ = 2048
DEPTH = 2
DEC_BATCH = 128
DEC_SEQ = 1
PAST_LEN = 2048
PAGE_SIZE = 128

N_BRANCH = 3
BRANCH_W = D_MODEL // 2
CONV_W = BRANCH_W
CONV_K = 3
M_HEADS = 4
M_DK = BRANCH_W // (2 * M_HEADS)
M_DV = BRANCH_W // M_HEADS
M_CHUNK = 128
F_HEADS = 8
F_DH = BRANCH_W // F_HEADS
F_SCALE = F_DH ** -0.5
Q_BLOCK = 128
D_FF = ((8 * D_MODEL // 3 + 255) // 256) * 256
EPS = 1e-6

SPLIT_SIZES = (CONV_W, CONV_W, CONV_W,
               M_HEADS * M_DK, M_HEADS * M_DK, M_HEADS * M_DV, M_HEADS * M_DV, M_HEADS, M_HEADS,
               F_HEADS * F_DH, F_HEADS * F_DH, F_HEADS * F_DH, F_HEADS,
               N_BRANCH * D_MODEL)
N_IN = sum(SPLIT_SIZES)

kernel_name = 'hybrid_conv_mlstm_fox_macaron_step'


def _split_points():
    pts, acc = [], 0
    for s in SPLIT_SIZES[:-1]:
        acc += s
        pts.append(acc)
    return pts


def rms_norm(x, w):
    xf = x.astype(jnp.float32)
    y = xf * lax.rsqrt(jnp.mean(xf * xf, axis=-1, keepdims=True) + EPS)
    return (y * w.astype(jnp.float32)).astype(x.dtype)


def swiglu(x, w_gate, w_up, w_down):
    return (jax.nn.silu(x @ w_gate) * (x @ w_up)) @ w_down


def short_conv(u, w, buf):
    full = jnp.concatenate([buf.astype(u.dtype), u], axis=1)
    y = lax.conv_general_dilated(full, w.astype(u.dtype)[:, None, :], window_strides=(1,), padding='VALID',
                                 dimension_numbers=('NWC', 'WIO', 'NWC'), feature_group_count=u.shape[-1])
    return y, full[:, full.shape[1] - (CONV_K - 1):]


def mlstm_chunk(carry, xs):
    c, n, m = carry
    q, k, v, ig, lf = xs
    L = q.shape[1]
    bt = jnp.swapaxes(jnp.cumsum(lf, axis=1), 1, 2)
    igt = jnp.swapaxes(ig, 1, 2)
    causal = jnp.tril(jnp.ones((L, L), dtype=bool))
    dmat = jnp.where(causal, bt[:, :, :, None] - bt[:, :, None, :] + igt[:, :, None, :], -jnp.inf)
    inter = bt + m[:, :, None]
    m_t = jnp.maximum(inter, jnp.max(dmat, axis=-1))
    w = jnp.exp(dmat - m_t[..., None])
    a = jnp.exp(inter - m_t)
    s = jnp.einsum('nthd,nshd->nhts', q, k) * w
    num = a[..., None] * jnp.einsum('nthd,nhde->nhte', q, c) + jnp.einsum('nhts,nshe->nhte', s, v)
    den = a * jnp.einsum('nthd,nhd->nht', q, n) + jnp.sum(s, axis=-1)
    h = num / jnp.maximum(jnp.abs(den), jnp.exp(-m_t))[..., None]
    w_last = w[:, :, -1, :]
    a_last = a[:, :, -1]
    c_new = a_last[..., None, None] * c + jnp.einsum('nhs,nshd,nshe->nhde', w_last, k, v)
    n_new = a_last[..., None] * n + jnp.einsum('nhs,nshd->nhd', w_last, k)
    return (c_new, n_new, m_t[:, :, -1]), jnp.swapaxes(h, 1, 2)


def mlstm(q, k, v, ig, lf, state):
    n_seq, t = q.shape[:2]
    L = M_CHUNK if t % M_CHUNK == 0 else t
    nc = t // L

    def to_chunks(a):
        return jnp.moveaxis(a.astype(jnp.float32).reshape((n_seq, nc, L) + a.shape[2:]), 1, 0)

    init = (state[0].astype(jnp.float32), state[1].astype(jnp.float32), state[2].astype(jnp.float32))
    new_state, h = lax.scan(mlstm_chunk, init, (to_chunks(q), to_chunks(k), to_chunks(v), to_chunks(ig), to_chunks(lf)))
    return jnp.moveaxis(h, 0, 1).reshape(n_seq, t, M_HEADS, M_DV), new_state


def fox_prompt(q, k, v, lf):
    t = q.shape[1]
    f = jnp.swapaxes(jnp.cumsum(lf.astype(jnp.float32), axis=1), 1, 2)
    key_pos = jnp.arange(t)

    def block(i):
        t0 = i * Q_BLOCK
        qb = lax.dynamic_slice_in_dim(q, t0, Q_BLOCK, axis=1)
        fb = lax.dynamic_slice_in_dim(f, t0, Q_BLOCK, axis=2)
        s = jnp.einsum('nqhd,nkhd->nhqk', qb, k).astype(jnp.float32) * F_SCALE
        s = s + fb[..., :, None] - f[..., None, :]
        mask = (t0 + jnp.arange(Q_BLOCK))[:, None] >= key_pos[None, :]
        p = jax.nn.softmax(jnp.where(mask, s, -jnp.inf), axis=-1)
        return jnp.einsum('nhqk,nkhd->nqhd', p.astype(v.dtype), v)

    out = lax.map(block, jnp.arange(t // Q_BLOCK))
    return jnp.moveaxis(out, 0, 1).reshape(q.shape[:2] + v.shape[2:])


def fox_decode(q, k, v, lf, past_k, past_v, past_lf):
    s_len = q.shape[1]
    p_len = past_k.shape[1]
    fp = jnp.cumsum(past_lf.astype(jnp.float32), axis=1)
    suffix = jnp.swapaxes(fp[:, -1:, :] - fp, 1, 2)
    fn = jnp.swapaxes(jnp.cumsum(lf.astype(jnp.float32), axis=1), 1, 2)
    s_past = jnp.einsum('nqhd,nkhd->nhqk', q, past_k.astype(q.dtype)).astype(jnp.float32) * F_SCALE
    s_past = s_past + fn[..., :, None] + suffix[..., None, :]
    s_new = jnp.einsum('nqhd,nkhd->nhqk', q, k).astype(jnp.float32) * F_SCALE + fn[..., :, None] - fn[..., None, :]
    s_new = jnp.where(jnp.tril(jnp.ones((s_len, s_len), dtype=bool)), s_new, -jnp.inf)
    p = jax.nn.softmax(jnp.concatenate([s_past, s_new], axis=-1), axis=-1).astype(v.dtype)
    return (jnp.einsum('nhqk,nkhd->nqhd', p[..., :p_len], past_v.astype(v.dtype))
            + jnp.einsum('nhqk,nkhd->nqhd', p[..., p_len:], v))


def layer(x, lw, conv_buf, mstate, fox_attend):
    (n1, g1, u1, d1, nm, w_in, conv_w, m_bi, m_bf, m_norm, f_bf, f_qn, f_kn, w_br, w_o, n2, g2, u2, d2) = lw
    x = x + 0.5 * swiglu(rms_norm(x, n1), g1, u1, d1)
    h = rms_norm(x, nm)
    n_seq, t = h.shape[:2]
    z = h @ w_in
    (cu, cb, cc, mq, mk, mv, mo, mi, mf, fq, fk, fv, ff, gz) = jnp.split(z, _split_points(), axis=-1)
    y_c, new_buf = short_conv(cc * cu, conv_w, conv_buf)
    y_conv = cb * y_c
    hm, new_mstate = mlstm(mq.reshape(n_seq, t, M_HEADS, M_DK),
                           mk.reshape(n_seq, t, M_HEADS, M_DK) * (M_DK ** -0.5),
                           mv.reshape(n_seq, t, M_HEADS, M_DV),
                           mi + m_bi,
                           jax.nn.log_sigmoid((mf + m_bf).astype(jnp.float32)),
                           mstate)
    hm = rms_norm(hm, m_norm.reshape(M_HEADS, M_DV)).reshape(n_seq, t, BRANCH_W).astype(x.dtype)
    y_m = jax.nn.sigmoid(mo) * hm
    qf = rms_norm(fq.reshape(n_seq, t, F_HEADS, F_DH), f_qn)
    kf = rms_norm(fk.reshape(n_seq, t, F_HEADS, F_DH), f_kn)
    vf = fv.reshape(n_seq, t, F_HEADS, F_DH)
    lff = jax.nn.log_sigmoid((ff + f_bf).astype(jnp.float32))
    y_f = fox_attend(qf, kf, vf, lff).reshape(n_seq, t, BRANCH_W)
    branches = jnp.stack([y_conv, y_m, y_f], axis=2)
    proj = jnp.einsum('ntbc,bcd->ntbd', branches, w_br)
    gates = jax.nn.sigmoid(gz.reshape(n_seq, t, N_BRANCH, D_MODEL))
    x = x + jnp.sum(gates * proj, axis=2) @ w_o
    x = x + 0.5 * swiglu(rms_norm(x, n2), g2, u2, d2)
    return x, new_buf, new_mstate, (kf, vf, lff)


def setup_inputs(seed: int = 0) -> dict:
    key = jax.random.key(seed)
    keys = list(jax.random.split(key, 40))
    f32 = jnp.float32
    n_pages = PAST_LEN // PAGE_SIZE
    n_pool = (DEC_BATCH * n_pages * 5) // 4

    def nrm(shape, scale):
        return jax.random.normal(keys.pop(), shape, f32) * scale

    def gain(shape):
        return 1.0 + nrm(shape, 0.02)

    x_prompt = nrm((BATCH, SEQ, D_MODEL), 1.0)
    x_sample = nrm((DEC_BATCH, DEC_SEQ, D_MODEL), 1.0)
    cache_k = nrm((DEPTH, n_pool, PAGE_SIZE, F_HEADS, F_DH), 1.0)
    cache_v = nrm((DEPTH, n_pool, PAGE_SIZE, F_HEADS, F_DH), 1.0)
    cache_logf = jax.nn.log_sigmoid(nrm((DEPTH, n_pool, PAGE_SIZE, F_HEADS), 1.0) + 4.0)
    page_table = jax.random.permutation(keys.pop(), n_pool)[:DEC_BATCH * n_pages].reshape(DEC_BATCH, n_pages).astype(jnp.int32)
    state_conv = nrm((DEPTH, DEC_BATCH, CONV_K - 1, CONV_W), 1.0)
    state_C = nrm((DEPTH, DEC_BATCH, M_HEADS, M_DK, M_DV), 0.1)
    state_n = nrm((DEPTH, DEC_BATCH, M_HEADS, M_DK), 0.1)
    state_m = jax.random.uniform(keys.pop(), (DEPTH, DEC_BATCH, M_HEADS), f32, 0.0, 2.0)
    return {
        'x_prompt': x_prompt,
        'x_sample': x_sample,
        'cache_k': cache_k,
        'cache_v': cache_v,
        'cache_logf': cache_logf,
        'page_table': page_table,
        'state_conv': state_conv,
        'state_C': state_C,
        'state_n': state_n,
        'state_m': state_m,
        'norm_ffn1': gain((DEPTH, D_MODEL)),
        'ffn1_gate': nrm((DEPTH, D_MODEL, D_FF), D_MODEL ** -0.5),
        'ffn1_up': nrm((DEPTH, D_MODEL, D_FF), D_MODEL ** -0.5),
        'ffn1_down': nrm((DEPTH, D_FF, D_MODEL), D_FF ** -0.5),
        'norm_mix': gain((DEPTH, D_MODEL)),
        'w_in': nrm((DEPTH, D_MODEL, N_IN), D_MODEL ** -0.5),
        'conv_w': nrm((DEPTH, CONV_K, CONV_W), 0.5),
        'mlstm_b_i': -2.0 + nrm((DEPTH, M_HEADS), 0.1),
        'mlstm_b_f': jnp.linspace(3.0, 6.0, M_HEADS, dtype=f32)[None, :] + nrm((DEPTH, M_HEADS), 0.1),
        'mlstm_norm': gain((DEPTH, M_HEADS * M_DV)),
        'fox_b_f': jnp.linspace(2.0, 6.0, F_HEADS, dtype=f32)[None, :] + nrm((DEPTH, F_HEADS), 0.1),
        'fox_q_norm': gain((DEPTH, F_DH)),
        'fox_k_norm': gain((DEPTH, F_DH)),
        'w_branch': nrm((DEPTH, N_BRANCH, BRANCH_W, D_MODEL), BRANCH_W ** -0.5),
        'w_out': nrm((DEPTH, D_MODEL, D_MODEL), D_MODEL ** -0.5),
        'norm_ffn2': gain((DEPTH, D_MODEL)),
        'ffn2_gate': nrm((DEPTH, D_MODEL, D_FF), D_MODEL ** -0.5),
        'ffn2_up': nrm((DEPTH, D_MODEL, D_FF), D_MODEL ** -0.5),
        'ffn2_down': nrm((DEPTH, D_FF, D_MODEL), D_FF ** -0.5),
    }


def reference(x_prompt, x_sample, cache_k, cache_v, cache_logf, page_table, state_conv, state_C, state_n, state_m,
              norm_ffn1, ffn1_gate, ffn1_up, ffn1_down, norm_mix, w_in, conv_w, mlstm_b_i, mlstm_b_f, mlstm_norm,
              fox_b_f, fox_q_norm, fox_k_norm, w_branch, w_out, norm_ffn2, ffn2_gate, ffn2_up, ffn2_down):
    n_dec = page_table.shape[0]
    past_len = page_table.shape[1] * PAGE_SIZE
    n_pr = x_prompt.shape[0]
    xp, xs = x_prompt, x_sample
    kp, vp, lp, ks, vs, ls = [], [], [], [], [], []
    cbp, cbs, cp, np_, mp, cs, ns, ms = [], [], [], [], [], [], [], []
    for l in range(DEPTH):
        lw = (norm_ffn1[l], ffn1_gate[l], ffn1_up[l], ffn1_down[l], norm_mix[l], w_in[l], conv_w[l],
              mlstm_b_i[l], mlstm_b_f[l], mlstm_norm[l], fox_b_f[l], fox_q_norm[l], fox_k_norm[l],
              w_branch[l], w_out[l], norm_ffn2[l], ffn2_gate[l], ffn2_up[l], ffn2_down[l])
        zero_buf = jnp.zeros((n_pr, CONV_K - 1, CONV_W), xp.dtype)
        zero_m = (jnp.zeros((n_pr, M_HEADS, M_DK, M_DV), jnp.float32),
                  jnp.zeros((n_pr, M_HEADS, M_DK), jnp.float32),
                  jnp.zeros((n_pr, M_HEADS), jnp.float32))
        xp, buf_p, ms_p, rows_p = layer(xp, lw, zero_buf, zero_m, fox_prompt)
        pk = cache_k[l][page_table].reshape(n_dec, past_len, F_HEADS, F_DH)
        pv = cache_v[l][page_table].reshape(n_dec, past_len, F_HEADS, F_DH)
        plf = cache_logf[l][page_table].reshape(n_dec, past_len, F_HEADS)
        attend = functools.partial(fox_decode, past_k=pk, past_v=pv, past_lf=plf)
        xs, buf_s, ms_s, rows_s = layer(xs, lw, state_conv[l], (state_C[l], state_n[l], state_m[l]), attend)
        kp.append(rows_p[0]); vp.append(rows_p[1]); lp.append(rows_p[2])
        ks.append(rows_s[0]); vs.append(rows_s[1]); ls.append(rows_s[2])
        cbp.append(buf_p); cbs.append(buf_s)
        cp.append(ms_p[0]); np_.append(ms_p[1]); mp.append(ms_p[2])
        cs.append(ms_s[0]); ns.append(ms_s[1]); ms.append(ms_s[2])
    return (xp, xs,
            jnp.stack(kp), jnp.stack(vp), jnp.stack(lp),
            jnp.stack(ks), jnp.stack(vs), jnp.stack(ls),
            jnp.stack(cbp), jnp.stack(cbs),
            jnp.stack(cp), jnp.stack(np_), jnp.stack(mp),
            jnp.stack(cs), jnp.stack(ns), jnp.stack(ms))
```

```python
import functools

import jax
import jax.numpy as jnp
from jax import lax
from jax.experimental import pallas as pl
from jax.experimental.pallas import tpu as pltpu

F32 = jnp.float32
BF16 = jnp.bfloat16
EPS = 1e-6
LANES = 128
VMEM_LIMIT = 56 * 1024 * 1024
M_CHUNK = 128
NT_DIMS = (((1,), (1,)), ((), ()))


def _params(*sem):
    return pltpu.CompilerParams(dimension_semantics=sem, vmem_limit_bytes=VMEM_LIMIT)


def _blk(n, pref):
    if n <= pref:
        return n
    b = pref
    while n % b:
        b //= 2
    return b


def _rms(x, w):
    return x * lax.rsqrt(jnp.mean(x * x, axis=-1, keepdims=True) + EPS) * w


def _dot(a, b):
    return jnp.dot(a, b, preferred_element_type=F32)


def _log_sigmoid(x):
    return jnp.minimum(x, 0.0) - jnp.log(1.0 + jnp.exp(-jnp.abs(x)))


def _ffn_kernel(x_ref, n_ref, wg_ref, wu_ref, wd_ref, nn_ref, o_ref, *rest, emit_next):
    if emit_next:
        hn_ref, h_scr = rest
    else:
        (h_scr,) = rest
    j = pl.program_id(1)

    @pl.when(j == 0)
    def _():
        x = x_ref[...]
        h_scr[...] = _rms(x, n_ref[...]).astype(BF16)
        o_ref[...] = x

    h = h_scr[...]
    g = _dot(h, wg_ref[...])
    u = _dot(h, wu_ref[...])
    a = (0.5 * g * jax.nn.sigmoid(g) * u).astype(BF16)
    o_ref[...] += _dot(a, wd_ref[...])

    if emit_next:
        @pl.when(j == pl.num_programs(1) - 1)
        def _():
            hn_ref[...] = _rms(o_ref[...], nn_ref[...]).astype(BF16)


def _ffn(x, norm, wg, wu, wd, next_norm, emit_next, bm_pref=512, bf_pref=512):
    m, d = x.shape
    f = wg.shape[1]
    bm, bf = _blk(m, bm_pref), _blk(f, bf_pref)
    out_shape = [jax.ShapeDtypeStruct((m, d), F32)]
    out_specs = [pl.BlockSpec((bm, d), lambda i, j: (i, 0))]
    if emit_next:
        out_shape.append(jax.ShapeDtypeStruct((m, d), BF16))
        out_specs.append(pl.BlockSpec((bm, d), lambda i, j: (i, 0)))
    res = pl.pallas_call(
        functools.partial(_ffn_kernel, emit_next=emit_next),
        grid=(m // bm, f // bf),
        in_specs=[
            pl.BlockSpec((bm, d), lambda i, j: (i, 0)),
            pl.BlockSpec((1, d), lambda i, j: (0, 0)),
            pl.BlockSpec((d, bf), lambda i, j: (0, j)),
            pl.BlockSpec((d, bf), lambda i, j: (0, j)),
            pl.BlockSpec((bf, d), lambda i, j: (j, 0)),
            pl.BlockSpec((1, d), lambda i, j: (0, 0)),
        ],
        out_specs=out_specs,
        out_shape=out_shape,
        scratch_shapes=[pltpu.VMEM((bm, d), BF16)],
        name="swiglu_block",
        compiler_params=_params("parallel", "arbitrary"),
    )(x, norm, wg, wu, wd, next_norm)
    return res if emit_next else res[0]


def _mm_kernel(*refs, act, has_scale, has_res):
    h_ref, w_ref = refs[0], refs[1]
    k = 2
    acc = _dot(h_ref[...], w_ref[...])
    if has_scale:
        acc = acc * refs[k][...]
        k += 1
    if act == "sigmoid":
        acc = jax.nn.sigmoid(acc)
    if has_res:
        acc = refs[k][...] + acc
        k += 1
    o_ref = refs[k]
    o_ref[...] = acc.astype(o_ref.dtype)


def _mm(h, w, out_dtype, act=None, scale=None, res=None, bm_pref=1024, bn_pref=1024):
    m, kd = h.shape
    n = w.shape[1]
    bm, bn = _blk(m, bm_pref), _blk(n, bn_pref)
    args = [h, w]
    in_specs = [pl.BlockSpec((bm, kd), lambda j, i: (i, 0)),
                pl.BlockSpec((kd, bn), lambda j, i: (0, j))]
    if scale is not None:
        args.append(scale)
        in_specs.append(pl.BlockSpec((1, bn), lambda j, i: (0, j)))
    if res is not None:
        args.append(res)
        in_specs.append(pl.BlockSpec((bm, bn), lambda j, i: (i, j)))
    return pl.pallas_call(
        functools.partial(_mm_kernel, act=act, has_scale=scale is not None, has_res=res is not None),
        grid=(n // bn, m // bm),
        in_specs=in_specs,
        out_specs=pl.BlockSpec((bm, bn), lambda j, i: (i, j)),
        out_shape=jax.ShapeDtypeStruct((m, n), out_dtype),
        name="proj_" + (act or "lin") + ("_res" if res is not None else ""),
        compiler_params=_params("parallel", "parallel"),
    )(*args)


def _conv_prompt_kernel(h_ref, wu_ref, wb_ref, wc_ref, cw_ref, y_ref, tail_ref, carry_scr,
                        *, blocks_per_seq):
    i = pl.program_id(1)

    @pl.when(i % blocks_per_seq == 0)
    def _():
        carry_scr[...] = jnp.zeros_like(carry_scr)

    h = h_ref[...]
    u = _dot(h, wc_ref[...]) * _dot(h, wu_ref[...])
    cb = _dot(h, wb_ref[...])
    bm = u.shape[0]
    row = lax.broadcasted_iota(jnp.int32, u.shape, 0)
    c1 = carry_scr[7:8, :]
    c2 = carry_scr[6:7, :]
    u1 = jnp.where(row >= 1, pltpu.roll(u, 1, 0), c1)
    u2 = jnp.where(row >= 2, pltpu.roll(u, 2, 0), jnp.where(row == 0, c2, c1))
    cw = cw_ref[...]
    y = cb * (cw[0:1, :] * u2 + cw[1:2, :] * u1 + cw[2:3, :] * u)
    y_ref[...] = y.astype(BF16)
    carry_scr[...] = u[bm - 8:, :]
    tail_ref[0] = u[bm - 2:, :]


def _conv_prompt(h, w_conv, conv_w, n_seq, bm_pref=1024, bn_pref=512):
    m, kd = h.shape
    c = conv_w.shape[1]
    t = m // n_seq
    bm, bn = _blk(t, bm_pref), _blk(c, bn_pref)
    nb = c // bn
    bps = t // bm
    return pl.pallas_call(
        functools.partial(_conv_prompt_kernel, blocks_per_seq=bps),
        grid=(nb, m // bm),
        in_specs=[
            pl.BlockSpec((bm, kd), lambda j, i: (i, 0)),
            pl.BlockSpec((kd, bn), lambda j, i: (0, j)),
            pl.BlockSpec((kd, bn), lambda j, i: (0, j + nb)),
            pl.BlockSpec((kd, bn), lambda j, i: (0, j + 2 * nb)),
            pl.BlockSpec((conv_w.shape[0], bn), lambda j, i: (0, j)),
        ],
        out_specs=[
            pl.BlockSpec((bm, bn), lambda j, i: (i, j)),
            pl.BlockSpec((1, 2, bn), lambda j, i: (i // bps, 0, j)),
        ],
        out_shape=[jax.ShapeDtypeStruct((m, c), BF16),
                   jax.ShapeDtypeStruct((n_seq, 2, c), F32)],
        scratch_shapes=[pltpu.VMEM((8, bn), F32)],
        name="conv_prompt",
        compiler_params=_params("parallel", "arbitrary"),
    )(h, w_conv, w_conv, w_conv, conv_w)


def _conv_decode_kernel(h_ref, wu_ref, wb_ref, wc_ref, cw_ref, b0_ref, b1_ref, y_ref, u_ref):
    h = h_ref[...]
    u = _dot(h, wc_ref[...]) * _dot(h, wu_ref[...])
    cb = _dot(h, wb_ref[...])
    cw = cw_ref[...]
    y = cb * (cw[0:1, :] * b0_ref[...] + cw[1:2, :] * b1_ref[...] + cw[2:3, :] * u)
    y_ref[...] = y.astype(BF16)
    u_ref[...] = u


def _conv_decode(h, w_conv, conv_w, buf0, buf1, bn_pref=512):
    m, kd = h.shape
    c = conv_w.shape[1]
    bn = _blk(c, bn_pref)
    nb = c // bn
    return pl.pallas_call(
        _conv_decode_kernel,
        grid=(nb,),
        in_specs=[
            pl.BlockSpec((m, kd), lambda j: (0, 0)),
            pl.BlockSpec((kd, bn), lambda j: (0, j)),
            pl.BlockSpec((kd, bn), lambda j: (0, j + nb)),
            pl.BlockSpec((kd, bn), lambda j: (0, j + 2 * nb)),
            pl.BlockSpec((conv_w.shape[0], bn), lambda j: (0, j)),
            pl.BlockSpec((m, bn), lambda j: (0, j)),
            pl.BlockSpec((m, bn), lambda j: (0, j)),
        ],
        out_specs=[pl.BlockSpec((m, bn), lambda j: (0, j)),
                   pl.BlockSpec((m, bn), lambda j: (0, j))],
        out_shape=[jax.ShapeDtypeStruct((m, c), BF16), jax.ShapeDtypeStruct((m, c), F32)],
        name="conv_decode",
        compiler_params=_params("parallel"),
    )(h, w_conv, w_conv, w_conv, conv_w, buf0, buf1)


def _fox_in_kernel(h_ref, wq_ref, wk_ref, wv_ref, qn_ref, kn_ref,
                   q_ref, kb_ref, vb_ref, kf_ref, vf_ref, *, dh, q_scale):
    h = h_ref[...]
    fq = _dot(h, wq_ref[...])
    fk = _dot(h, wk_ref[...])
    fv = _dot(h, wv_ref[...])
    for hh in range(fq.shape[1] // dh):
        sl = slice(hh * dh, (hh + 1) * dh)
        qh = _rms(fq[:, sl], qn_ref[...])
        kh = _rms(fk[:, sl], kn_ref[...])
        q_ref[:, sl] = (qh * q_scale).astype(BF16)
        kf_ref[:, sl] = kh
        kb_ref[:, sl] = kh.astype(BF16)
    vf_ref[...] = fv
    vb_ref[...] = fv.astype(BF16)


def _fox_in(h, w_fox, qn, kn, dh, bm_pref=1024, bn_pref=512):
    m, kd = h.shape
    c = w_fox.shape[1] // 3
    bm, bn = _blk(m, bm_pref), _blk(c, bn_pref)
    nb = c // bn
    o_spec = pl.BlockSpec((bm, bn), lambda j, i: (i, j))
    return pl.pallas_call(
        functools.partial(_fox_in_kernel, dh=dh, q_scale=dh ** -0.5),
        grid=(nb, m // bm),
        in_specs=[
            pl.BlockSpec((bm, kd), lambda j, i: (i, 0)),
            pl.BlockSpec((kd, bn), lambda j, i: (0, j)),
            pl.BlockSpec((kd, bn), lambda j, i: (0, j + nb)),
            pl.BlockSpec((kd, bn), lambda j, i: (0, j + 2 * nb)),
            pl.BlockSpec((1, dh), lambda j, i: (0, 0)),
            pl.BlockSpec((1, dh), lambda j, i: (0, 0)),
        ],
        out_specs=[o_spec] * 5,
        out_shape=[jax.ShapeDtypeStruct((m, c), BF16)] * 3 + [jax.ShapeDtypeStruct((m, c), F32)] * 2,
        name="fox_qkv",
        compiler_params=_params("parallel", "parallel"),
    )(h, w_fox, w_fox, w_fox, qn, kn)


def _mlstm_prompt_kernel(qkv_ref, og_ref, z_ref, b_ref, mn_ref,
                         y_ref, c_ref, n_ref, m_ref, *, nh, dk, dv):
    @pl.when(pl.program_id(1) == 0)
    def _():
        c_ref[...] = jnp.zeros_like(c_ref)
        n_ref[...] = jnp.zeros_like(n_ref)
        m_ref[...] = jnp.zeros_like(m_ref)

    z = z_ref[...] + b_ref[...]
    L = z.shape[0]
    row = lax.broadcasted_iota(jnp.int32, z.shape, 0)
    lane = lax.broadcasted_iota(jnp.int32, z.shape, 1)
    bt = _log_sigmoid(z)
    s = 1
    while s < L:
        bt = bt + jnp.where(row >= s, pltpu.roll(bt, s, 0), 0.0)
        s *= 2
    g = jnp.where(lane < nh, z, bt)
    g_t = g.T
    causal = (lax.broadcasted_iota(jnp.int32, (L, L), 0)
              >= lax.broadcasted_iota(jnp.int32, (L, L), 1))
    for h in range(nh):
        q = qkv_ref[:, h * dk:(h + 1) * dk]
        k = qkv_ref[:, nh * dk + h * dk: nh * dk + (h + 1) * dk]
        v = qkv_ref[:, 2 * nh * dk + h * dv: 2 * nh * dk + (h + 1) * dv]
        ig_col, bt_col = g[:, h:h + 1], g[:, nh + h:nh + h + 1]
        ig_row, bt_row = g_t[h:h + 1, :], g_t[nh + h:nh + h + 1, :]
        m_prev = m_ref[0, h:h + 1, 0:1]
        dmat = jnp.where(causal, bt_col - bt_row + ig_row, -jnp.inf)
        inter = bt_col + m_prev
        m_t = jnp.maximum(inter, jnp.max(dmat, axis=1, keepdims=True))
        w = jnp.exp(dmat - m_t)
        a = jnp.exp(inter - m_t)
        sc = lax.dot_general(q, k, NT_DIMS, preferred_element_type=F32) * w
        c_old = c_ref[0, h]
        n_old = n_ref[0, h:h + 1, :]
        num = a * _dot(q, c_old.astype(BF16)) + _dot(sc.astype(BF16), v)
        den = (a * jnp.sum(q.astype(F32) * n_old, axis=1, keepdims=True)
               + jnp.sum(sc, axis=1, keepdims=True))
        hm = num / jnp.maximum(jnp.abs(den), jnp.exp(-m_t))
        m_last = m_t[L - 1:L, :]
        a_last = a[L - 1:L, :]
        w_last = jnp.exp(bt_col[L - 1:L, :] - bt_col + ig_col - m_last)
        kw = k.astype(F32) * w_last
        c_ref[0, h] = a_last * c_old + _dot(kw.T.astype(BF16), v)
        n_ref[0, h:h + 1, :] = a_last * n_old + jnp.sum(kw, axis=0, keepdims=True)
        m_ref[0, h:h + 1, :] = jnp.broadcast_to(m_last, (1, m_ref.shape[2]))
        vs = slice(h * dv, (h + 1) * dv)
        y_ref[:, vs] = (og_ref[:, vs].astype(F32) * _rms(hm, mn_ref[:, vs])).astype(BF16)


def _mlstm_prompt(qkv, og, z, bias, m_norm, n_seq, nh, dk, dv):
    m = qkv.shape[0]
    t = m // n_seq
    L = M_CHUNK if t % M_CHUNK == 0 else t
    nc = t // L
    bw = nh * dv
    return pl.pallas_call(
        functools.partial(_mlstm_prompt_kernel, nh=nh, dk=dk, dv=dv),
        grid=(n_seq, nc),
        in_specs=[
            pl.BlockSpec((L, qkv.shape[1]), lambda n, c: (n * nc + c, 0)),
            pl.BlockSpec((L, bw), lambda n, c: (n * nc + c, 0)),
            pl.BlockSpec((L, LANES), lambda n, c: (n * nc + c, 0)),
            pl.BlockSpec((1, LANES), lambda n, c: (0, 0)),
            pl.BlockSpec((1, bw), lambda n, c: (0, 0)),
        ],
        out_specs=[
            pl.BlockSpec((L, bw), lambda n, c: (n * nc + c, 0)),
            pl.BlockSpec((1, nh, dk, dv), lambda n, c: (n, 0, 0, 0)),
            pl.BlockSpec((1, nh, dk), lambda n, c: (n, 0, 0)),
            pl.BlockSpec((1, 8, LANES), lambda n, c: (n, 0, 0)),
        ],
        out_shape=[
            jax.ShapeDtypeStruct((m, bw), BF16),
            jax.ShapeDtypeStruct((n_seq, nh, dk, dv), F32),
            jax.ShapeDtypeStruct((n_seq, nh, dk), F32),
            jax.ShapeDtypeStruct((n_seq, 8, LANES), F32),
        ],
        name="mlstm_prompt",
        compiler_params=_params("parallel", "arbitrary"),
    )(qkv, og, z, bias, m_norm)


def _mlstm_decode_kernel(qkv_ref, og_ref, z_ref, b_ref, mn_ref, c_ref, n_ref, m_ref,
                         y_ref, co_ref, no_ref, mo_ref, *, nh, dk, dv):
    bs = qkv_ref.shape[0]
    qkv = qkv_ref[...].astype(F32)
    og = og_ref[...].astype(F32)
    z = z_ref[...] + b_ref[...]
    lf_all = _log_sigmoid(z)
    eye = (lax.broadcasted_iota(jnp.int32, (dk, dk), 0)
           == lax.broadcasted_iota(jnp.int32, (dk, dk), 1))
    for s in range(bs):
        for h in range(nh):
            q = qkv[s:s + 1, h * dk:(h + 1) * dk]
            k = qkv[s:s + 1, nh * dk + h * dk: nh * dk + (h + 1) * dk]
            v = qkv[s:s + 1, 2 * nh * dk + h * dv: 2 * nh * dk + (h + 1) * dv]
            ig = z[s:s + 1, h:h + 1]
            lf = lf_all[s:s + 1, nh + h:nh + h + 1]
            m_old = m_ref[s:s + 1, h:h + 1]
            c_old = c_ref[s, h]
            n_old = n_ref[s, h:h + 1, :]
            inter = lf + m_old
            m_t = jnp.maximum(inter, ig)
            w = jnp.exp(ig - m_t)
            a = jnp.exp(inter - m_t)
            sc = jnp.sum(q * k, axis=1, keepdims=True) * w
            num = a * _dot(q.astype(BF16), c_old.astype(BF16)) + sc * v
            den = a * jnp.sum(q * n_old, axis=1, keepdims=True) + sc
            hm = num / jnp.maximum(jnp.abs(den), jnp.exp(-m_t))
            k_col = jnp.sum(jnp.where(eye, k, 0.0), axis=1, keepdims=True)
            co_ref[s, h] = a * c_old + (w * k_col) * v
            no_ref[s, h:h + 1, :] = a * n_old + w * k
            mo_ref[s:s + 1, h:h + 1] = m_t
            vs = slice(h * dv, (h + 1) * dv)
            y_ref[s:s + 1, vs] = (og[s:s + 1, vs] * _rms(hm, mn_ref[:, vs])).astype(BF16)


def _mlstm_decode(qkv, og, z, bias, m_norm, c, n, m_state, nh, dk, dv):
    nd = qkv.shape[0]
    bs = _blk(nd, 8)
    bw = nh * dv
    row = lambda i: (i, 0)
    return pl.pallas_call(
        functools.partial(_mlstm_decode_kernel, nh=nh, dk=dk, dv=dv),
        grid=(nd // bs,),
        in_specs=[
            pl.BlockSpec((bs, qkv.shape[1]), row),
            pl.BlockSpec((bs, bw), row),
            pl.BlockSpec((bs, LANES), row),
            pl.BlockSpec((1, LANES), lambda i: (0, 0)),
            pl.BlockSpec((1, bw), lambda i: (0, 0)),
            pl.BlockSpec((bs, nh, dk, dv), lambda i: (i, 0, 0, 0)),
            pl.BlockSpec((bs, nh, dk), lambda i: (i, 0, 0)),
            pl.BlockSpec((bs, nh), row),
        ],
        out_specs=[
            pl.BlockSpec((bs, bw), row),
            pl.BlockSpec((bs, nh, dk, dv), lambda i: (i, 0, 0, 0)),
            pl.BlockSpec((bs, nh, dk), lambda i: (i, 0, 0)),
            pl.BlockSpec((bs, nh), row),
        ],
        out_shape=[
            jax.ShapeDtypeStruct((nd, bw), BF16),
            jax.ShapeDtypeStruct(c.shape, F32),
            jax.ShapeDtypeStruct(n.shape, F32),
            jax.ShapeDtypeStruct(m_state.shape, F32),
        ],
        name="mlstm_decode",
        compiler_params=_params("parallel"),
    )(qkv, og, z, bias, m_norm, c, n, m_state)


def _fgate_kernel(z_ref, b_ref, lf_ref, cum_ref):
    lf = _log_sigmoid(z_ref[...] + b_ref[...])
    lf_ref[...] = lf
    t = lf.shape[0]
    row = lax.broadcasted_iota(jnp.int32, lf.shape, 0)
    acc = lf
    s = 1
    while s < t:
        acc = acc + jnp.where(row >= s, pltpu.roll(acc, s, 0), 0.0)
        s *= 2
    cum_ref[...] = acc


def _fgate(z, bias, n_seq):
    m = z.shape[0]
    t = m // n_seq
    spec = pl.BlockSpec((t, LANES), lambda n: (n, 0))
    return pl.pallas_call(
        _fgate_kernel,
        grid=(n_seq,),
        in_specs=[spec, pl.BlockSpec((1, LANES), lambda n: (0, 0))],
        out_specs=[spec, spec],
        out_shape=[jax.ShapeDtypeStruct((m, LANES), F32)] * 2,
        name="fox_gate_cumsum",
        compiler_params=_params("parallel"),
    )(z, bias)


def _attn_kernel(q_ref, k_ref, v_ref, fc_ref, fr_ref, o_ref, m_scr, l_scr, acc_scr,
                 *, nh, dh, gate_col):
    i, j = pl.program_id(1), pl.program_id(2)
    bq, bk = q_ref.shape[0], k_ref.shape[0]

    @pl.when(j == 0)
    def _():
        m_scr[...] = jnp.full_like(m_scr, -jnp.inf)
        l_scr[...] = jnp.zeros_like(l_scr)
        acc_scr[...] = jnp.zeros_like(acc_scr)

    @pl.when(j * bk <= i * bq + bq - 1)
    def _():
        qpos = i * bq + lax.broadcasted_iota(jnp.int32, (bq, bk), 0)
        kpos = j * bk + lax.broadcasted_iota(jnp.int32, (bq, bk), 1)
        mask = qpos >= kpos
        for h in range(nh):
            sl = slice(h * dh, (h + 1) * dh)
            s = lax.dot_general(q_ref[:, sl], k_ref[:, sl], NT_DIMS, preferred_element_type=F32)
            s = s + fc_ref[:, gate_col + h:gate_col + h + 1] - fr_ref[h:h + 1, :]
            s = jnp.where(mask, s, -jnp.inf)
            m_prev = m_scr[h]
            m_new = jnp.maximum(m_prev, jnp.max(s, axis=1, keepdims=True))
            p = jnp.exp(s - m_new)
            alpha = jnp.exp(m_prev - m_new)
            l_scr[h] = alpha * l_scr[h] + jnp.sum(p, axis=1, keepdims=True)
            acc_scr[:, sl] = alpha * acc_scr[:, sl] + _dot(p.astype(BF16), v_ref[:, sl])
            m_scr[h] = m_new

    @pl.when(j == pl.num_programs(2) - 1)
    def _():
        for h in range(nh):
            sl = slice(h * dh, (h + 1) * dh)
            o_ref[:, sl] = (acc_scr[:, sl] / l_scr[h]).astype(BF16)


def _attn(q, k, v, f_col, f_row, n_seq, nh, dh, gate_col, blk_pref=512):
    m, c = q.shape
    t = m // n_seq
    bq = bk = _blk(t, blk_pref)
    nq = nk = t // bq

    def kv_map(b, i, j):
        return (b * nk + jnp.minimum(j, i), 0)

    return pl.pallas_call(
        functools.partial(_attn_kernel, nh=nh, dh=dh, gate_col=gate_col),
        grid=(n_seq, nq, nk),
        in_specs=[
            pl.BlockSpec((bq, c), lambda b, i, j: (b * nq + i, 0)),
            pl.BlockSpec((bk, c), kv_map),
            pl.BlockSpec((bk, c), kv_map),
            pl.BlockSpec((bq, LANES), lambda b, i, j: (b * nq + i, 0)),
            pl.BlockSpec((f_row.shape[0], bk), lambda b, i, j: (0, b * nk + jnp.minimum(j, i))),
        ],
        out_specs=pl.BlockSpec((bq, c), lambda b, i, j: (b * nq + i, 0)),
        out_shape=jax.ShapeDtypeStruct((m, c), BF16),
        scratch_shapes=[pltpu.VMEM((nh, bq, 1), F32), pltpu.VMEM((nh, bq, 1), F32),
                        pltpu.VMEM((bq, c), F32)],
        name="fox_prompt_attn",
        compiler_params=_params("parallel", "parallel", "arbitrary"),
    )(q, k, v, f_col, f_row)


def _decode_attn_kernel(pt_ref, q_ref, kn_ref, vn_ref, z_ref, b_ref, k_ref, v_ref, lf_ref,
                        o_ref, lfn_ref, m_scr, l_scr, acc_scr, carry_scr, *, nh, dh, gate_col):
    p = pl.program_id(1)
    c = nh * dh
    head_of_lane = lax.broadcasted_iota(jnp.int32, (nh, c), 1) // dh
    head_mask = head_of_lane == lax.broadcasted_iota(jnp.int32, (nh, c), 0)
    q_rows = jnp.where(head_mask, jnp.broadcast_to(q_ref[0].astype(F32), (nh, c)), 0.0)
    lf_new = _log_sigmoid(z_ref[0] + b_ref[...])
    lfn_ref[0] = lf_new
    pick = (lax.broadcasted_iota(jnp.int32, (nh, LANES), 1)
            == lax.broadcasted_iota(jnp.int32, (nh, LANES), 0) + gate_col)
    fn_col = jnp.sum(jnp.where(pick, jnp.broadcast_to(lf_new, (nh, LANES)), 0.0),
                     axis=1, keepdims=True)

    @pl.when(p == 0)
    def _():
        m_scr[...] = jnp.full_like(m_scr, -jnp.inf)
        l_scr[...] = jnp.zeros_like(l_scr)
        acc_scr[...] = jnp.zeros_like(acc_scr)
        carry_scr[...] = jnp.zeros_like(carry_scr)

    s = lax.dot_general(q_rows.astype(BF16), k_ref[...].astype(BF16), NT_DIMS,
                        preferred_element_type=F32)
    lf = lf_ref[...]
    page = lf.shape[1]
    pos = lax.broadcasted_iota(jnp.int32, lf.shape, 1)
    pre = lf
    sh = 1
    while sh < page:
        pre = pre + jnp.where(pos >= sh, pltpu.roll(pre, sh, 1), 0.0)
        sh *= 2
    tot = pre[:, page - 1:page]
    s = s + fn_col + (tot - pre) + carry_scr[...]
    m_prev = m_scr[...]
    m_new = jnp.maximum(m_prev, jnp.max(s, axis=1, keepdims=True))
    pe = jnp.exp(s - m_new)
    alpha = jnp.exp(m_prev - m_new)
    l_scr[...] = alpha * l_scr[...] + jnp.sum(pe, axis=1, keepdims=True)
    acc_scr[...] = alpha * acc_scr[...] + _dot(pe.astype(BF16), v_ref[...].astype(BF16))
    m_scr[...] = m_new
    carry_scr[...] = carry_scr[...] + tot

    @pl.when(p == pl.num_programs(1) - 1)
    def _():
        s_new = jnp.sum(q_rows * kn_ref[0], axis=1, keepdims=True)
        m_fin = jnp.maximum(m_scr[...], s_new)
        al = jnp.exp(m_scr[...] - m_fin)
        pn = jnp.exp(s_new - m_fin)
        l_fin = al * l_scr[...] + pn
        out = (al * acc_scr[...] + pn * vn_ref[0]) / l_fin
        o_ref[0] = jnp.sum(jnp.where(head_mask, out, 0.0), axis=0, keepdims=True).astype(BF16)


def _decode_attn(page_table, q, k_new, v_new, z, bias, cache_k, cache_v, cache_lf_t,
                 layer, nh, dh, gate_col):
    nd, c = q.shape
    n_pages = page_table.shape[1]
    page = cache_k.shape[2]
    row3 = lambda n, p, pt: (n, 0, 0)
    cache_map = lambda n, p, pt: (layer, pt[n, n_pages - 1 - p], 0, 0)
    grid_spec = pltpu.PrefetchScalarGridSpec(
        num_scalar_prefetch=1,
        grid=(nd, n_pages),
        in_specs=[
            pl.BlockSpec((1, 1, c), row3),
            pl.BlockSpec((1, 1, c), row3),
            pl.BlockSpec((1, 1, c), row3),
            pl.BlockSpec((1, 1, LANES), row3),
            pl.BlockSpec((1, LANES), lambda n, p, pt: (0, 0)),
            pl.BlockSpec((None, None, page, c), cache_map),
            pl.BlockSpec((None, None, page, c), cache_map),
            pl.BlockSpec((None, None, nh, page), cache_map),
        ],
        out_specs=[pl.BlockSpec((1, 1, c), row3), pl.BlockSpec((1, 1, LANES), row3)],
        scratch_shapes=[pltpu.VMEM((nh, 1), F32), pltpu.VMEM((nh, 1), F32),
                        pltpu.VMEM((nh, c), F32), pltpu.VMEM((nh, 1), F32)],
    )
    out, lf_new = pl.pallas_call(
        functools.partial(_decode_attn_kernel, nh=nh, dh=dh, gate_col=gate_col),
        grid_spec=grid_spec,
        out_shape=[jax.ShapeDtypeStruct((nd, 1, c), BF16),
                   jax.ShapeDtypeStruct((nd, 1, LANES), F32)],
        name="fox_decode_attn",
        compiler_params=_params("parallel", "arbitrary"),
    )(page_table, q.reshape(nd, 1, c), k_new.reshape(nd, 1, c), v_new.reshape(nd, 1, c),
      z.reshape(nd, 1, LANES), bias, cache_k, cache_v, cache_lf_t)
    return out.reshape(nd, c), lf_new.reshape(nd, LANES)


def _merge_kernel(h_ref, yc_ref, ym_ref, yf_ref, wg_ref, wb_ref, o_ref, acc_scr):
    b = pl.program_id(2)
    gate = jax.nn.sigmoid(_dot(h_ref[...], wg_ref[...]))

    @pl.when(b == 0)
    def _():
        acc_scr[...] = gate * _dot(yc_ref[...], wb_ref[...])

    @pl.when(b == 1)
    def _():
        acc_scr[...] += gate * _dot(ym_ref[...], wb_ref[...])

    @pl.when(b == 2)
    def _():
        o_ref[...] = (acc_scr[...] + gate * _dot(yf_ref[...], wb_ref[...])).astype(BF16)


def _merge(h, yc, ym, yf, w_gate, w_branch, bm_pref=1024, bn_pref=1024):
    m, d = h.shape
    bw = yc.shape[1]
    bm, bn = _blk(m, bm_pref), _blk(d, bn_pref)
    nc = d // bn
    y_spec = pl.BlockSpec((bm, bw), lambda i, c, b: (i, 0))
    return pl.pallas_call(
        _merge_kernel,
        grid=(m // bm, nc, 3),
        in_specs=[
            pl.BlockSpec((bm, d), lambda i, c, b: (i, 0)),
            y_spec, y_spec, y_spec,
            pl.BlockSpec((d, bn), lambda i, c, b: (0, b * nc + c)),
            pl.BlockSpec((None, bw, bn), lambda i, c, b: (b, 0, c)),
        ],
        out_specs=pl.BlockSpec((bm, bn), lambda i, c, b: (i, c)),
        out_shape=jax.ShapeDtypeStruct((m, d), BF16),
        scratch_shapes=[pltpu.VMEM((bm, bn), F32)],
        name="gated_merge",
        compiler_params=_params("parallel", "parallel", "arbitrary"),
    )(h, yc, ym, yf, w_gate, w_branch)


def kernel(x_prompt, x_sample, cache_k, cache_v, cache_logf, page_table, state_conv, state_C, state_n, state_m, norm_ffn1, ffn1_gate, ffn1_up, ffn1_down, norm_mix, w_in, conv_w, mlstm_b_i, mlstm_b_f, mlstm_norm, fox_b_f, fox_q_norm, fox_k_norm, w_branch, w_out, norm_ffn2, ffn2_gate, ffn2_up, ffn2_down):
    depth = w_in.shape[0]
    n_pr, t_pr, d = x_prompt.shape
    n_dec = x_sample.shape[0]
    conv_c = conv_w.shape[2]
    nh_m, dk, dv = state_C.shape[2:]
    n_pool, page, nh_f, dh = cache_k.shape[1:]
    fox_c = nh_f * dh
    assert x_sample.shape[1] == 1, "sample group is one new token per sequence"
    assert 2 * nh_m + nh_f <= LANES

    o_conv = 0
    o_mq = 3 * conv_c
    o_mo = o_mq + 2 * nh_m * dk + nh_m * dv
    o_mi = o_mo + nh_m * dv
    o_fq = o_mi + 2 * nh_m
    o_ff = o_fq + 3 * fox_c
    o_gz = o_ff + nh_f
    gate_col = 2 * nh_m

    xp = x_prompt.reshape(n_pr * t_pr, d)
    xs = x_sample.reshape(n_dec, d)
    ck = cache_k.reshape(depth, n_pool, page, fox_c)
    cv = cache_v.reshape(depth, n_pool, page, fox_c)
    clf_t = jnp.swapaxes(cache_logf, 2, 3)

    mqkv_scale = jnp.concatenate([jnp.ones((nh_m * dk,), F32),
                                  jnp.full((nh_m * dk,), dk ** -0.5, F32),
                                  jnp.ones((nh_m * dv,), F32)])[None, :]
    outs = {k: [] for k in ("kp", "vp", "lp", "ks", "vs", "ls", "cbp", "cbs",
                            "cp", "np", "mp", "cs", "ns", "ms")}
    for l in range(depth):
        bf = lambda a: a.astype(BF16)
        wi = w_in[l]
        w_conv = bf(wi[:, o_conv:o_mq])
        w_mqkv = bf(wi[:, o_mq:o_mo])
        w_mo = bf(wi[:, o_mo:o_mi])
        w_small = bf(jnp.concatenate(
            [wi[:, o_mi:o_fq], wi[:, o_ff:o_gz],
             jnp.zeros((d, LANES - 2 * nh_m - nh_f), F32)], axis=1))
        w_fox = bf(wi[:, o_fq:o_ff])
        w_gate = bf(wi[:, o_gz:])
        w_br, w_o = bf(w_branch[l]), bf(w_out[l])
        g1, u1, d1 = bf(ffn1_gate[l]), bf(ffn1_up[l]), bf(ffn1_down[l])
        g2, u2, d2 = bf(ffn2_gate[l]), bf(ffn2_up[l]), bf(ffn2_down[l])
        n1, nm, n2 = norm_ffn1[l][None, :], norm_mix[l][None, :], norm_ffn2[l][None, :]
        small_bias = jnp.concatenate(
            [mlstm_b_i[l], mlstm_b_f[l], fox_b_f[l],
             jnp.zeros((LANES - 2 * nh_m - nh_f,), F32)])[None, :]
        m_norm = mlstm_norm[l][None, :]
        qn, kn = fox_q_norm[l][None, :], fox_k_norm[l][None, :]
        cw = conv_w[l]

        xp, hp = _ffn(xp, n1, g1, u1, d1, nm, True)
        yc_p, tail_p = _conv_prompt(hp, w_conv, cw, n_pr)
        qkv_p = _mm(hp, w_mqkv, BF16, scale=mqkv_scale)
        og_p = _mm(hp, w_mo, BF16, act="sigmoid")
        z_p = _mm(hp, w_small, F32)
        q_p, kb_p, vb_p, kf_p, vf_p = _fox_in(hp, w_fox, qn, kn, dh)
        ym_p, c_p, n_p, m_p = _mlstm_prompt(qkv_p, og_p, z_p, small_bias, m_norm, n_pr, nh_m, dk, dv)
        lf_p, cum_p = _fgate(z_p, small_bias, n_pr)
        f_row = cum_p[:, gate_col:gate_col + nh_f].T
        yf_p = _attn(q_p, kb_p, vb_p, cum_p, f_row, n_pr, nh_f, dh, gate_col)
        mg_p = _merge(hp, yc_p, ym_p, yf_p, w_gate, w_br)
        xp = _mm(mg_p, w_o, F32, res=xp)
        xp = _ffn(xp, n2, g2, u2, d2, n2, False)

        xs, hs = _ffn(xs, n1, g1, u1, d1, nm, True)
        yc_s, u_s = _conv_decode(hs, w_conv, cw, state_conv[l, :, 0, :], state_conv[l, :, 1, :])
        qkv_s = _mm(hs, w_mqkv, BF16, scale=mqkv_scale)
        og_s = _mm(hs, w_mo, BF16, act="sigmoid")
        z_s = _mm(hs, w_small, F32)
        q_s, _, _, kf_s, vf_s = _fox_in(hs, w_fox, qn, kn, dh)
        ym_s, c_s, n_s, m_s = _mlstm_decode(qkv_s, og_s, z_s, small_bias, m_norm,
                                            state_C[l], state_n[l], state_m[l], nh_m, dk, dv)
        yf_s, lf_s = _decode_attn(page_table, q_s, kf_s, vf_s, z_s, small_bias, ck, cv, clf_t,
                                  l, nh_f, dh, gate_col)
        mg_s = _merge(hs, yc_s, ym_s, yf_s, w_gate, w_br)
        xs = _mm(mg_s, w_o, F32, res=xs)
        xs = _ffn(xs, n2, g2, u2, d2, n2, False)

        outs["kp"].append(kf_p.reshape(n_pr, t_pr, nh_f, dh))
        outs["vp"].append(vf_p.reshape(n_pr, t_pr, nh_f, dh))
        outs["lp"].append(lf_p[:, gate_col:gate_col + nh_f].reshape(n_pr, t_pr, nh_f))
        outs["ks"].append(kf_s.reshape(n_dec, 1, nh_f, dh))
        outs["vs"].append(vf_s.reshape(n_dec, 1, nh_f, dh))
        outs["ls"].append(lf_s[:, gate_col:gate_col + nh_f].reshape(n_dec, 1, nh_f))
        outs["cbp"].append(tail_p)
        outs["cbs"].append(jnp.stack([state_conv[l, :, 1, :], u_s], axis=1))
        outs["cp"].append(c_p)
        outs["np"].append(n_p)
        outs["mp"].append(m_p[:, :nh_m, 0])
        outs["cs"].append(c_s)
        outs["ns"].append(n_s)
        outs["ms"].append(m_s)

    st = {k: jnp.stack(v) for k, v in outs.items()}
    return (xp.reshape(n_pr, t_pr, d), xs.reshape(n_dec, 1, d),
            st["kp"], st["vp"], st["lp"], st["ks"], st["vs"], st["ls"],
            st["cbp"], st["cbs"], st["cp"], st["np"], st["mp"], st["cs"], st["ns"], st["ms"])
```

```python
import functools

import jax
import jax.numpy as jnp
from jax import lax
from jax.experimental import pallas as pl
from jax.experimental.pallas import tpu as pltpu

F32 = jnp.float32
BF16 = jnp.bfloat16
EPS = 1e-6
LANES = 128
VMEM_LIMIT = 56 * 1024 * 1024
M_CHUNK = 128
NT_DIMS = (((1,), (1,)), ((), ()))
LOG2E = 1.4426950408889634


def _params(*sem):
    return pltpu.CompilerParams(dimension_semantics=sem, vmem_limit_bytes=VMEM_LIMIT)


def _blk(n, pref):
    if n <= pref:
        return n
    b = pref
    while n % b:
        b //= 2
    return b


def _rms(x, w):
    return x * lax.rsqrt(jnp.mean(x * x, axis=-1, keepdims=True) + EPS) * w


def _dot(a, b):
    return jnp.dot(a, b, preferred_element_type=F32)


def _log_sigmoid(x):
    return jnp.minimum(x, 0.0) - jnp.log(1.0 + jnp.exp(-jnp.abs(x)))


def _ffn_kernel(x_ref, n_ref, wg_ref, wu_ref, wd_ref, nn_ref, o_ref, *rest, emit_next, emit_bf16):
    rest = list(rest)
    hn_ref = rest.pop(0) if emit_next else None
    wb_refs = [rest.pop(0) for _ in range(3)] if emit_bf16 else None
    (h_scr,) = rest
    j = pl.program_id(1)

    @pl.when(j == 0)
    def _():
        x = x_ref[...]
        h_scr[...] = _rms(x, n_ref[...]).astype(BF16)
        o_ref[...] = x

    wg, wu, wd = wg_ref[...], wu_ref[...], wd_ref[...]
    if emit_bf16:
        wg, wu, wd = wg.astype(BF16), wu.astype(BF16), wd.astype(BF16)
        for ref, val in zip(wb_refs, (wg, wu, wd)):
            ref[...] = val
    h = h_scr[...]
    g = _dot(h, wg)
    u = _dot(h, wu)
    a = (0.5 * g * jax.nn.sigmoid(g) * u).astype(BF16)
    o_ref[...] += _dot(a, wd)

    if emit_next:
        @pl.when(j == pl.num_programs(1) - 1)
        def _():
            hn_ref[...] = _rms(o_ref[...], nn_ref[...]).astype(BF16)


def _ffn(x, norm, wg, wu, wd, next_norm, emit_next, emit_bf16=False, bm_pref=512, bf_pref=512):
    m, d = x.shape
    f = wg.shape[1]
    bm, bf = _blk(m, bm_pref), _blk(f, bf_pref)
    row = pl.BlockSpec((bm, d), lambda i, j: (i, 0))
    w_specs = [pl.BlockSpec((d, bf), lambda i, j: (0, j)),
               pl.BlockSpec((d, bf), lambda i, j: (0, j)),
               pl.BlockSpec((bf, d), lambda i, j: (j, 0))]
    out_shape = [jax.ShapeDtypeStruct((m, d), F32)]
    out_specs = [row]
    if emit_next:
        out_shape.append(jax.ShapeDtypeStruct((m, d), BF16))
        out_specs.append(row)
    if emit_bf16:
        assert m == bm, "each weight block must be visited exactly once"
        out_shape += [jax.ShapeDtypeStruct(w.shape, BF16) for w in (wg, wu, wd)]
        out_specs += w_specs
    return pl.pallas_call(
        functools.partial(_ffn_kernel, emit_next=emit_next, emit_bf16=emit_bf16),
        grid=(m // bm, f // bf),
        in_specs=[row, pl.BlockSpec((1, d), lambda i, j: (0, 0))] + w_specs
                 + [pl.BlockSpec((1, d), lambda i, j: (0, 0))],
        out_specs=out_specs,
        out_shape=out_shape,
        scratch_shapes=[pltpu.VMEM((bm, d), BF16)],
        name="swiglu_block",
        compiler_params=_params("parallel", "arbitrary"),
    )(x, norm, wg, wu, wd, next_norm)


def _mm_kernel(*refs, act, has_scale, has_res):
    h_ref, w_ref = refs[0], refs[1]
    k = 2
    acc = _dot(h_ref[...], w_ref[...])
    if has_scale:
        acc = acc * refs[k][...]
        k += 1
    if act == "sigmoid":
        acc = jax.nn.sigmoid(acc)
    if has_res:
        acc = refs[k][...] + acc
        k += 1
    o_ref = refs[k]
    o_ref[...] = acc.astype(o_ref.dtype)


def _mm(h, w, out_dtype, act=None, scale=None, res=None, bm_pref=1024, bn_pref=1024):
    m, kd = h.shape
    n = w.shape[1]
    bm, bn = _blk(m, bm_pref), _blk(n, bn_pref)
    args = [h, w]
    in_specs = [pl.BlockSpec((bm, kd), lambda j, i: (i, 0)),
                pl.BlockSpec((kd, bn), lambda j, i: (0, j))]
    if scale is not None:
        args.append(scale)
        in_specs.append(pl.BlockSpec((1, bn), lambda j, i: (0, j)))
    if res is not None:
        args.append(res)
        in_specs.append(pl.BlockSpec((bm, bn), lambda j, i: (i, j)))
    return pl.pallas_call(
        functools.partial(_mm_kernel, act=act, has_scale=scale is not None, has_res=res is not None),
        grid=(n // bn, m // bm),
        in_specs=in_specs,
        out_specs=pl.BlockSpec((bm, bn), lambda j, i: (i, j)),
        out_shape=jax.ShapeDtypeStruct((m, n), out_dtype),
        name="proj_" + (act or "lin") + ("_res" if res is not None else ""),
        compiler_params=_params("parallel", "parallel"),
    )(*args)


def _conv_prompt_kernel(h_ref, wu_ref, wb_ref, wc_ref, cw_ref, y_ref, tail_ref, carry_scr,
                        *, blocks_per_seq):
    i = pl.program_id(1)

    @pl.when(i % blocks_per_seq == 0)
    def _():
        carry_scr[...] = jnp.zeros_like(carry_scr)

    h = h_ref[...]
    u = _dot(h, wc_ref[...]) * _dot(h, wu_ref[...])
    cb = _dot(h, wb_ref[...])
    bm = u.shape[0]
    row = lax.broadcasted_iota(jnp.int32, u.shape, 0)
    c1 = carry_scr[7:8, :]
    c2 = carry_scr[6:7, :]
    u1 = jnp.where(row >= 1, pltpu.roll(u, 1, 0), c1)
    u2 = jnp.where(row >= 2, pltpu.roll(u, 2, 0), jnp.where(row == 0, c2, c1))
    cw = cw_ref[...]
    y = cb * (cw[0:1, :] * u2 + cw[1:2, :] * u1 + cw[2:3, :] * u)
    y_ref[...] = y.astype(BF16)
    carry_scr[...] = u[bm - 8:, :]
    tail_ref[0] = u[bm - 2:, :]


def _conv_prompt(h, w_conv, conv_w, n_seq, bm_pref=1024, bn_pref=512):
    m, kd = h.shape
    c = conv_w.shape[1]
    t = m // n_seq
    bm, bn = _blk(t, bm_pref), _blk(c, bn_pref)
    nb = c // bn
    bps = t // bm
    return pl.pallas_call(
        functools.partial(_conv_prompt_kernel, blocks_per_seq=bps),
        grid=(nb, m // bm),
        in_specs=[
            pl.BlockSpec((bm, kd), lambda j, i: (i, 0)),
            pl.BlockSpec((kd, bn), lambda j, i: (0, j)),
            pl.BlockSpec((kd, bn), lambda j, i: (0, j + nb)),
            pl.BlockSpec((kd, bn), lambda j, i: (0, j + 2 * nb)),
            pl.BlockSpec((conv_w.shape[0], bn), lambda j, i: (0, j)),
        ],
        out_specs=[
            pl.BlockSpec((bm, bn), lambda j, i: (i, j)),
            pl.BlockSpec((1, 2, bn), lambda j, i: (i // bps, 0, j)),
        ],
        out_shape=[jax.ShapeDtypeStruct((m, c), BF16),
                   jax.ShapeDtypeStruct((n_seq, 2, c), F32)],
        scratch_shapes=[pltpu.VMEM((8, bn), F32)],
        name="conv_prompt",
        compiler_params=_params("parallel", "arbitrary"),
    )(h, w_conv, w_conv, w_conv, conv_w)


def _conv_decode_kernel(h_ref, wu_ref, wb_ref, wc_ref, cw_ref, b0_ref, b1_ref, y_ref, u_ref):
    h = h_ref[...]
    u = _dot(h, wc_ref[...]) * _dot(h, wu_ref[...])
    cb = _dot(h, wb_ref[...])
    cw = cw_ref[...]
    y = cb * (cw[0:1, :] * b0_ref[...] + cw[1:2, :] * b1_ref[...] + cw[2:3, :] * u)
    y_ref[...] = y.astype(BF16)
    u_ref[...] = u


def _conv_decode(h, w_conv, conv_w, buf0, buf1, bn_pref=512):
    m, kd = h.shape
    c = conv_w.shape[1]
    bn = _blk(c, bn_pref)
    nb = c // bn
    return pl.pallas_call(
        _conv_decode_kernel,
        grid=(nb,),
        in_specs=[
            pl.BlockSpec((m, kd), lambda j: (0, 0)),
            pl.BlockSpec((kd, bn), lambda j: (0, j)),
            pl.BlockSpec((kd, bn), lambda j: (0, j + nb)),
            pl.BlockSpec((kd, bn), lambda j: (0, j + 2 * nb)),
            pl.BlockSpec((conv_w.shape[0], bn), lambda j: (0, j)),
            pl.BlockSpec((m, bn), lambda j: (0, j)),
            pl.BlockSpec((m, bn), lambda j: (0, j)),
        ],
        out_specs=[pl.BlockSpec((m, bn), lambda j: (0, j)),
                   pl.BlockSpec((m, bn), lambda j: (0, j))],
        out_shape=[jax.ShapeDtypeStruct((m, c), BF16), jax.ShapeDtypeStruct((m, c), F32)],
        name="conv_decode",
        compiler_params=_params("parallel"),
    )(h, w_conv, w_conv, w_conv, conv_w, buf0, buf1)


def _fox_in_kernel(h_ref, wq_ref, wk_ref, wv_ref, qn_ref, kn_ref, *rest, dh, q_scale):
    q_ref, kb_ref, vb_ref, kf_ref, vf_ref = rest[-5:]
    h = h_ref[...]
    fq = _dot(h, wq_ref[...])
    fk = _dot(h, wk_ref[...])
    fv = _dot(h, wv_ref[...])
    for hh in range(fq.shape[1] // dh):
        sl = slice(hh * dh, (hh + 1) * dh)
        qh = _rms(fq[:, sl], qn_ref[...])
        kh = _rms(fk[:, sl], kn_ref[...])
        q_ref[:, sl] = (qh * q_scale).astype(BF16)
        kf_ref[:, sl] = kh
        kb_ref[:, sl] = kh.astype(BF16)
    vf_ref[...] = fv
    vb_ref[...] = fv.astype(BF16)


def _fox_in(h, w_fox, qn, kn, dh, layer, depth, stacks=None, bm_pref=1024, bn_pref=512):
    m, kd = h.shape
    c = w_fox.shape[1] // 3
    bm, bn = _blk(m, bm_pref), _blk(c, bn_pref)
    nb = c // bn
    o_spec = pl.BlockSpec((bm, bn), lambda j, i: (i, j))
    s_spec = pl.BlockSpec((None, bm, bn), lambda j, i: (layer, i, j))
    args = [h, w_fox, w_fox, w_fox, qn, kn]
    in_specs = [
        pl.BlockSpec((bm, kd), lambda j, i: (i, 0)),
        pl.BlockSpec((kd, bn), lambda j, i: (0, j)),
        pl.BlockSpec((kd, bn), lambda j, i: (0, j + nb)),
        pl.BlockSpec((kd, bn), lambda j, i: (0, j + 2 * nb)),
        pl.BlockSpec((1, dh), lambda j, i: (0, 0)),
        pl.BlockSpec((1, dh), lambda j, i: (0, 0)),
    ]
    aliases = {}
    if stacks is not None:
        aliases = {len(args): 3, len(args) + 1: 4}
        args += list(stacks)
        in_specs += [pl.BlockSpec(memory_space=pl.ANY)] * 2
    return pl.pallas_call(
        functools.partial(_fox_in_kernel, dh=dh, q_scale=dh ** -0.5 * LOG2E),
        grid=(nb, m // bm),
        in_specs=in_specs,
        out_specs=[o_spec] * 3 + [s_spec] * 2,
        out_shape=[jax.ShapeDtypeStruct((m, c), BF16)] * 3
                  + [jax.ShapeDtypeStruct((depth, m, c), F32)] * 2,
        input_output_aliases=aliases,
        name="fox_qkv",
        compiler_params=_params("parallel", "parallel"),
    )(*args)


def _mlstm_prompt_kernel(qkv_ref, og_ref, z_ref, b_ref, mn_ref,
                         y_ref, c_ref, n_ref, m_ref, *, nh, dk, dv):
    @pl.when(pl.program_id(1) == 0)
    def _():
        c_ref[...] = jnp.zeros_like(c_ref)
        n_ref[...] = jnp.zeros_like(n_ref)
        m_ref[...] = jnp.zeros_like(m_ref)

    z = z_ref[...] + b_ref[...]
    L = z.shape[0]
    row = lax.broadcasted_iota(jnp.int32, z.shape, 0)
    lane = lax.broadcasted_iota(jnp.int32, z.shape, 1)
    bt = _log_sigmoid(z)
    s = 1
    while s < L:
        bt = bt + jnp.where(row >= s, pltpu.roll(bt, s, 0), 0.0)
        s *= 2
    g = jnp.where(lane < nh, z, bt)
    g_t = g.T
    causal = (lax.broadcasted_iota(jnp.int32, (L, L), 0)
              >= lax.broadcasted_iota(jnp.int32, (L, L), 1))
    for h in range(nh):
        q = qkv_ref[:, h * dk:(h + 1) * dk]
        k = qkv_ref[:, nh * dk + h * dk: nh * dk + (h + 1) * dk]
        v = qkv_ref[:, 2 * nh * dk + h * dv: 2 * nh * dk + (h + 1) * dv]
        ig_col, bt_col = g[:, h:h + 1], g[:, nh + h:nh + h + 1]
        ig_row, bt_row = g_t[h:h + 1, :], g_t[nh + h:nh + h + 1, :]
        m_prev = m_ref[0, h:h + 1, 0:1]
        dmat = jnp.where(causal, bt_col - bt_row + ig_row, -jnp.inf)
        inter = bt_col + m_prev
        m_t = jnp.maximum(inter, jnp.max(dmat, axis=1, keepdims=True))
        w = jnp.exp(dmat - m_t)
        a = jnp.exp(inter - m_t)
        sc = lax.dot_general(q, k, NT_DIMS, preferred_element_type=F32) * w
        c_old = c_ref[0, h]
        n_old = n_ref[0, h:h + 1, :]
        num = a * _dot(q, c_old.astype(BF16)) + _dot(sc.astype(BF16), v)
        den = (a * jnp.sum(q.astype(F32) * n_old, axis=1, keepdims=True)
               + jnp.sum(sc, axis=1, keepdims=True))
        hm = num / jnp.maximum(jnp.abs(den), jnp.exp(-m_t))
        m_last = m_t[L - 1:L, :]
        a_last = a[L - 1:L, :]
        w_last = jnp.exp(bt_col[L - 1:L, :] - bt_col + ig_col - m_last)
        kw = k.astype(F32) * w_last
        c_ref[0, h] = a_last * c_old + _dot(kw.T.astype(BF16), v)
        n_ref[0, h:h + 1, :] = a_last * n_old + jnp.sum(kw, axis=0, keepdims=True)
        m_ref[0, h:h + 1, :] = jnp.broadcast_to(m_last, (1, m_ref.shape[2]))
        vs = slice(h * dv, (h + 1) * dv)
        y_ref[:, vs] = (og_ref[:, vs].astype(F32) * _rms(hm, mn_ref[:, vs])).astype(BF16)


def _mlstm_prompt(qkv, og, z, bias, m_norm, n_seq, nh, dk, dv):
    m = qkv.shape[0]
    t = m // n_seq
    L = M_CHUNK if t % M_CHUNK == 0 else t
    nc = t // L
    bw = nh * dv
    return pl.pallas_call(
        functools.partial(_mlstm_prompt_kernel, nh=nh, dk=dk, dv=dv),
        grid=(n_seq, nc),
        in_specs=[
            pl.BlockSpec((L, qkv.shape[1]), lambda n, c: (n * nc + c, 0)),
            pl.BlockSpec((L, bw), lambda n, c: (n * nc + c, 0)),
            pl.BlockSpec((L, LANES), lambda n, c: (n * nc + c, 0)),
            pl.BlockSpec((1, LANES), lambda n, c: (0, 0)),
            pl.BlockSpec((1, bw), lambda n, c: (0, 0)),
        ],
        out_specs=[
            pl.BlockSpec((L, bw), lambda n, c: (n * nc + c, 0)),
            pl.BlockSpec((1, nh, dk, dv), lambda n, c: (n, 0, 0, 0)),
            pl.BlockSpec((1, nh, dk), lambda n, c: (n, 0, 0)),
            pl.BlockSpec((1, 8, LANES), lambda n, c: (n, 0, 0)),
        ],
        out_shape=[
            jax.ShapeDtypeStruct((m, bw), BF16),
            jax.ShapeDtypeStruct((n_seq, nh, dk, dv), F32),
            jax.ShapeDtypeStruct((n_seq, nh, dk), F32),
            jax.ShapeDtypeStruct((n_seq, 8, LANES), F32),
        ],
        name="mlstm_prompt",
        compiler_params=_params("parallel", "arbitrary"),
    )(qkv, og, z, bias, m_norm)


def _mlstm_decode_kernel(qkv_ref, og_ref, z_ref, b_ref, mn_ref, c_ref, n_ref, m_ref,
                         *rest, nh, dk, dv):
    y_ref, co_ref, no_ref, mo_ref = rest[-4:]
    bs = qkv_ref.shape[0]
    qkv = qkv_ref[...].astype(F32)
    og = og_ref[...].astype(F32)
    z = z_ref[...] + b_ref[...]
    lf_all = _log_sigmoid(z)
    eye = (lax.broadcasted_iota(jnp.int32, (dk, dk), 0)
           == lax.broadcasted_iota(jnp.int32, (dk, dk), 1))
    for s in range(bs):
        for h in range(nh):
            q = qkv[s:s + 1, h * dk:(h + 1) * dk]
            k = qkv[s:s + 1, nh * dk + h * dk: nh * dk + (h + 1) * dk]
            v = qkv[s:s + 1, 2 * nh * dk + h * dv: 2 * nh * dk + (h + 1) * dv]
            ig = z[s:s + 1, h:h + 1]
            lf = lf_all[s:s + 1, nh + h:nh + h + 1]
            m_old = m_ref[s:s + 1, h:h + 1]
            c_old = c_ref[s, h]
            n_old = n_ref[s, h:h + 1, :]
            inter = lf + m_old
            m_t = jnp.maximum(inter, ig)
            w = jnp.exp(ig - m_t)
            a = jnp.exp(inter - m_t)
            sc = jnp.sum(q * k, axis=1, keepdims=True) * w
            num = a * _dot(q.astype(BF16), c_old.astype(BF16)) + sc * v
            den = a * jnp.sum(q * n_old, axis=1, keepdims=True) + sc
            hm = num / jnp.maximum(jnp.abs(den), jnp.exp(-m_t))
            k_col = jnp.sum(jnp.where(eye, k, 0.0), axis=1, keepdims=True)
            co_ref[s, h] = a * c_old + (w * k_col) * v
            no_ref[s, h:h + 1, :] = a * n_old + w * k
            mo_ref[s:s + 1, h:h + 1] = m_t
            vs = slice(h * dv, (h + 1) * dv)
            y_ref[s:s + 1, vs] = (og[s:s + 1, vs] * _rms(hm, mn_ref[:, vs])).astype(BF16)


def _mlstm_decode(qkv, og, z, bias, m_norm, c_all, n, m_state, layer, c_stack=None):
    nd = qkv.shape[0]
    nh, dk, dv = c_all.shape[2:]
    bs = _blk(nd, 8)
    bw = nh * dv
    row = lambda i: (i, 0)
    c_spec = pl.BlockSpec((None, bs, nh, dk, dv), lambda i: (layer, i, 0, 0, 0))
    args = [qkv, og, z, bias, m_norm, c_all, n, m_state]
    in_specs = [
        pl.BlockSpec((bs, qkv.shape[1]), row),
        pl.BlockSpec((bs, bw), row),
        pl.BlockSpec((bs, LANES), row),
        pl.BlockSpec((1, LANES), lambda i: (0, 0)),
        pl.BlockSpec((1, bw), lambda i: (0, 0)),
        c_spec,
        pl.BlockSpec((bs, nh, dk), lambda i: (i, 0, 0)),
        pl.BlockSpec((bs, nh), row),
    ]
    aliases = {}
    if c_stack is not None:
        aliases = {len(args): 1}
        args.append(c_stack)
        in_specs.append(pl.BlockSpec(memory_space=pl.ANY))
    return pl.pallas_call(
        functools.partial(_mlstm_decode_kernel, nh=nh, dk=dk, dv=dv),
        grid=(nd // bs,),
        in_specs=in_specs,
        out_specs=[
            pl.BlockSpec((bs, bw), row),
            c_spec,
            pl.BlockSpec((bs, nh, dk), lambda i: (i, 0, 0)),
            pl.BlockSpec((bs, nh), row),
        ],
        out_shape=[
            jax.ShapeDtypeStruct((nd, bw), BF16),
            jax.ShapeDtypeStruct(c_all.shape, F32),
            jax.ShapeDtypeStruct(n.shape, F32),
            jax.ShapeDtypeStruct(m_state.shape, F32),
        ],
        input_output_aliases=aliases,
        name="mlstm_decode",
        compiler_params=_params("parallel"),
    )(*args)


def _fgate_kernel(z_ref, b_ref, lf_ref, cum_ref):
    lf = _log_sigmoid(z_ref[...] + b_ref[...])
    lf_ref[...] = lf
    t = lf.shape[0]
    row = lax.broadcasted_iota(jnp.int32, lf.shape, 0)
    acc = lf
    s = 1
    while s < t:
        acc = acc + jnp.where(row >= s, pltpu.roll(acc, s, 0), 0.0)
        s *= 2
    cum_ref[...] = acc


def _fgate(z, bias, n_seq):
    m = z.shape[0]
    t = m // n_seq
    spec = pl.BlockSpec((t, LANES), lambda n: (n, 0))
    return pl.pallas_call(
        _fgate_kernel,
        grid=(n_seq,),
        in_specs=[spec, pl.BlockSpec((1, LANES), lambda n: (0, 0))],
        out_specs=[spec, spec],
        out_shape=[jax.ShapeDtypeStruct((m, LANES), F32)] * 2,
        name="fox_gate_cumsum",
        compiler_params=_params("parallel"),
    )(z, bias)


def _attn_kernel(q_ref, k_ref, v_ref, fc_ref, fr_ref, o_ref, m_scr, l_scr, acc_scr,
                 *, nh, dh, gate_col):
    i, j = pl.program_id(1), pl.program_id(2)
    bq, bk = q_ref.shape[0], k_ref.shape[0]
    nrep = bk // LANES

    @pl.when(j == 0)
    def _():
        m_scr[...] = jnp.full_like(m_scr, -jnp.inf)
        l_scr[...] = jnp.zeros_like(l_scr)
        acc_scr[...] = jnp.zeros_like(acc_scr)

    def update(on_diagonal):
        fq = fc_ref[...] * LOG2E
        fk = fr_ref[...] * LOG2E
        if on_diagonal:
            mask = (lax.broadcasted_iota(jnp.int32, (bq, bk), 0)
                    >= lax.broadcasted_iota(jnp.int32, (bq, bk), 1))
        for h in range(nh):
            sl = slice(h * dh, (h + 1) * dh)
            s = lax.dot_general(q_ref[:, sl], k_ref[:, sl], NT_DIMS, preferred_element_type=F32)
            s = s + fq[:, gate_col + h:gate_col + h + 1] - fk[h:h + 1, :]
            if on_diagonal:
                s = jnp.where(mask, s, -jnp.inf)
            m_prev = m_scr[h]
            m_new = jnp.maximum(m_prev, jnp.max(s, axis=1, keepdims=True))
            p = jnp.exp2(s - jnp.concatenate([m_new] * nrep, axis=1))
            alpha = jnp.exp2(m_prev - m_new)
            l_scr[h] = alpha * l_scr[h] + jnp.sum(p, axis=1, keepdims=True)
            acc_scr[:, sl] = alpha * acc_scr[:, sl] + _dot(p.astype(BF16), v_ref[:, sl])
            m_scr[h] = m_new

    @pl.when(j < i)
    def _():
        update(False)

    @pl.when(j == i)
    def _():
        update(True)

    @pl.when(j == pl.num_programs(2) - 1)
    def _():
        for h in range(nh):
            sl = slice(h * dh, (h + 1) * dh)
            o_ref[:, sl] = (acc_scr[:, sl] / l_scr[h]).astype(BF16)


def _attn(q, k, v, f_col, f_row, n_seq, nh, dh, gate_col, blk_pref=512):
    m, c = q.shape
    t = m // n_seq
    assert dh == LANES, "running softmax stats are kept one head-dim wide"
    bq = bk = _blk(t, blk_pref)
    nq = nk = t // bq

    def kv_map(b, i, j):
        return (b * nk + jnp.minimum(j, i), 0)

    return pl.pallas_call(
        functools.partial(_attn_kernel, nh=nh, dh=dh, gate_col=gate_col),
        grid=(n_seq, nq, nk),
        in_specs=[
            pl.BlockSpec((bq, c), lambda b, i, j: (b * nq + i, 0)),
            pl.BlockSpec((bk, c), kv_map),
            pl.BlockSpec((bk, c), kv_map),
            pl.BlockSpec((bq, LANES), lambda b, i, j: (b * nq + i, 0)),
            pl.BlockSpec((f_row.shape[0], bk), lambda b, i, j: (0, b * nk + jnp.minimum(j, i))),
        ],
        out_specs=pl.BlockSpec((bq, c), lambda b, i, j: (b * nq + i, 0)),
        out_shape=jax.ShapeDtypeStruct((m, c), BF16),
        scratch_shapes=[pltpu.VMEM((nh, bq, LANES), F32), pltpu.VMEM((nh, bq, LANES), F32),
                        pltpu.VMEM((bq, c), F32)],
        name="fox_prompt_attn",
        compiler_params=_params("parallel", "parallel", "arbitrary"),
    )(q, k, v, f_col, f_row)


def _decode_attn_kernel(pt_ref, q_ref, kn_ref, vn_ref, z_ref, b_ref, *rest,
                        nh, dh, gate_col, pps):
    k_refs, v_refs, lf_refs = rest[:pps], rest[pps:2 * pps], rest[2 * pps:3 * pps]
    o_ref, lfn_ref, lf_scr, m_scr, l_scr, acc_scr, carry_scr = rest[3 * pps:]
    step = pl.program_id(1)
    page = k_refs[0].shape[0]
    w = page * nh
    q = q_ref[0]
    qb = q.astype(BF16)
    lf_new = _log_sigmoid(z_ref[0] + b_ref[...])
    lfn_ref[0] = lf_new
    pick = (lax.broadcasted_iota(jnp.int32, (nh, LANES), 1)
            == lax.broadcasted_iota(jnp.int32, (nh, LANES), 0) + gate_col)
    fn_col = jnp.sum(jnp.where(pick, jnp.broadcast_to(lf_new, (nh, LANES)), 0.0),
                     axis=1, keepdims=True) * LOG2E
    own_head = (lax.broadcasted_iota(jnp.int32, (nh, w), 1) % nh
                == lax.broadcasted_iota(jnp.int32, (nh, w), 0))

    @pl.when(step == 0)
    def _():
        m_scr[...] = jnp.full_like(m_scr, -jnp.inf)
        l_scr[...] = jnp.zeros_like(l_scr)
        acc_scr[...] = jnp.zeros_like(acc_scr)
        carry_scr[...] = jnp.zeros_like(carry_scr)

    for r in range(pps):
        lf_scr[r:r + 1, :] = lf_refs[r][...]
    lf = lf_scr[...]
    pos = lax.broadcasted_iota(jnp.int32, lf.shape, 1)
    rowi = lax.broadcasted_iota(jnp.int32, lf.shape, 0)
    suf, tot = lf, lf
    sh = nh
    while sh < w:
        suf = suf + jnp.where(pos < w - sh, pltpu.roll(suf, w - sh, 1), 0.0)
        tot = tot + pltpu.roll(tot, sh, 1)
        sh *= 2
    newer = tot
    sh = 1
    while sh < pps:
        newer = newer + jnp.where(rowi >= sh, pltpu.roll(newer, sh, 0), 0.0)
        sh *= 2
    bias = (suf - lf + newer - tot + carry_scr[...]) * LOG2E

    scores = []
    for r in range(pps):
        k2 = k_refs[r][...].reshape(w, dh).astype(BF16)
        s = lax.dot_general(qb, k2, NT_DIMS, preferred_element_type=F32)
        scores.append(jnp.where(own_head, s + fn_col + bias[r:r + 1, :], -jnp.inf))
    m_prev = m_scr[...]
    m_new = jnp.maximum(m_prev, jnp.max(functools.reduce(jnp.maximum, scores),
                                        axis=1, keepdims=True))
    alpha = jnp.exp2(m_prev - m_new)
    l_run = alpha * l_scr[...]
    acc = alpha * acc_scr[...]
    for r in range(pps):
        pe = jnp.exp2(scores[r] - m_new)
        l_run = l_run + jnp.sum(pe, axis=1, keepdims=True)
        acc = acc + _dot(pe.astype(BF16), v_refs[r][...].reshape(w, dh).astype(BF16))
    m_scr[...] = m_new
    l_scr[...] = l_run
    acc_scr[...] = acc
    carry_scr[...] = carry_scr[...] + newer[pps - 1:pps, :]

    @pl.when(step == pl.num_programs(1) - 1)
    def _():
        s_new = jnp.sum(q * kn_ref[0], axis=1, keepdims=True)
        m_fin = jnp.maximum(m_new, s_new)
        al = jnp.exp2(m_new - m_fin)
        pn = jnp.exp2(s_new - m_fin)
        o_ref[0] = ((al * acc + pn * vn_ref[0]) / (al * l_run + pn)).astype(o_ref.dtype)


def _decode_attn(page_table, q, k_new, v_new, z, bias, cache_k, cache_v, cache_lf,
                 layer, gate_col, pps_pref=16):
    nd, nh, dh = q.shape
    n_pages = page_table.shape[1]
    page = cache_k.shape[2]
    w = page * nh
    assert (page & (page - 1)) == 0, "per-head scans over a page use power-of-two strides"
    pps = _blk(n_pages, pps_pref)
    assert (pps & (pps - 1)) == 0
    row3 = lambda n, p, pt: (n, 0, 0)

    def page_map(r, tail):
        return lambda n, p, pt: (layer, pt[n, n_pages - 1 - (p * pps + r)]) + tail

    kv_specs = [pl.BlockSpec((None, None, page, nh, dh), page_map(r, (0, 0, 0))) for r in range(pps)]
    lf_specs = [pl.BlockSpec((None, None, 1, w), page_map(r, (0, 0))) for r in range(pps)]
    grid_spec = pltpu.PrefetchScalarGridSpec(
        num_scalar_prefetch=1,
        grid=(nd, n_pages // pps),
        in_specs=[
            pl.BlockSpec((1, nh, dh), row3),
            pl.BlockSpec((1, nh, dh), row3),
            pl.BlockSpec((1, nh, dh), row3),
            pl.BlockSpec((1, 1, LANES), row3),
            pl.BlockSpec((1, LANES), lambda n, p, pt: (0, 0)),
        ] + kv_specs + kv_specs + lf_specs,
        out_specs=[pl.BlockSpec((1, nh, dh), row3), pl.BlockSpec((1, 1, LANES), row3)],
        scratch_shapes=[pltpu.VMEM((pps, w), F32), pltpu.VMEM((nh, 1), F32), pltpu.VMEM((nh, 1), F32),
                        pltpu.VMEM((nh, dh), F32), pltpu.VMEM((1, w), F32)],
    )
    out, lf_new = pl.pallas_call(
        functools.partial(_decode_attn_kernel, nh=nh, dh=dh, gate_col=gate_col, pps=pps),
        grid_spec=grid_spec,
        out_shape=[jax.ShapeDtypeStruct((nd, nh, dh), F32),
                   jax.ShapeDtypeStruct((nd, 1, LANES), F32)],
        name="fox_decode_attn",
        compiler_params=_params("parallel", "arbitrary"),
    )(page_table, q, k_new, v_new, z.reshape(nd, 1, LANES), bias,
      *([cache_k] * pps), *([cache_v] * pps), *([cache_lf] * pps))
    return out, lf_new.reshape(nd, LANES)


def _merge_kernel(h_ref, yc_ref, ym_ref, yf_ref, wg_ref, wb_ref, o_ref, acc_scr):
    b = pl.program_id(2)
    gate = jax.nn.sigmoid(_dot(h_ref[...], wg_ref[...]))

    @pl.when(b == 0)
    def _():
        acc_scr[...] = gate * _dot(yc_ref[...], wb_ref[...])

    @pl.when(b == 1)
    def _():
        acc_scr[...] += gate * _dot(ym_ref[...], wb_ref[...])

    @pl.when(b == 2)
    def _():
        o_ref[...] = (acc_scr[...] + gate * _dot(yf_ref[...], wb_ref[...])).astype(BF16)


def _merge(h, yc, ym, yf, w_gate, w_branch, bm_pref=1024, bn_pref=1024):
    m, d = h.shape
    bw = yc.shape[1]
    bm, bn = _blk(m, bm_pref), _blk(d, bn_pref)
    nc = d // bn
    y_spec = pl.BlockSpec((bm, bw), lambda i, c, b: (i, 0))
    return pl.pallas_call(
        _merge_kernel,
        grid=(m // bm, nc, 3),
        in_specs=[
            pl.BlockSpec((bm, d), lambda i, c, b: (i, 0)),
            y_spec, y_spec, y_spec,
            pl.BlockSpec((d, bn), lambda i, c, b: (0, b * nc + c)),
            pl.BlockSpec((None, bw, bn), lambda i, c, b: (b, 0, c)),
        ],
        out_specs=pl.BlockSpec((bm, bn), lambda i, c, b: (i, c)),
        out_shape=jax.ShapeDtypeStruct((m, d), BF16),
        scratch_shapes=[pltpu.VMEM((bm, bn), F32)],
        name="gated_merge",
        compiler_params=_params("parallel", "parallel", "arbitrary"),
    )(h, yc, ym, yf, w_gate, w_branch)


def kernel(x_prompt, x_sample, cache_k, cache_v, cache_logf, page_table, state_conv, state_C, state_n, state_m, norm_ffn1, ffn1_gate, ffn1_up, ffn1_down, norm_mix, w_in, conv_w, mlstm_b_i, mlstm_b_f, mlstm_norm, fox_b_f, fox_q_norm, fox_k_norm, w_branch, w_out, norm_ffn2, ffn2_gate, ffn2_up, ffn2_down):
    depth = w_in.shape[0]
    n_pr, t_pr, d = x_prompt.shape
    n_dec = x_sample.shape[0]
    conv_c = conv_w.shape[2]
    nh_m, dk, dv = state_C.shape[2:]
    n_pool, page, nh_f, dh = cache_k.shape[1:]
    fox_c = nh_f * dh
    assert x_sample.shape[1] == 1, "sample group is one new token per sequence"
    assert 2 * nh_m + nh_f <= LANES

    o_conv = 0
    o_mq = 3 * conv_c
    o_mo = o_mq + 2 * nh_m * dk + nh_m * dv
    o_mi = o_mo + nh_m * dv
    o_fq = o_mi + 2 * nh_m
    o_ff = o_fq + 3 * fox_c
    o_gz = o_ff + nh_f
    gate_col = 2 * nh_m

    xp = x_prompt.reshape(n_pr * t_pr, d)
    xs = x_sample.reshape(n_dec, d)
    clf = cache_logf.reshape(depth, n_pool, 1, page * nh_f)

    mqkv_scale = jnp.concatenate([jnp.ones((nh_m * dk,), F32),
                                  jnp.full((nh_m * dk,), dk ** -0.5, F32),
                                  jnp.ones((nh_m * dv,), F32)])[None, :]
    outs = {k: [] for k in ("lp", "ks", "vs", "ls", "cbp", "cbs", "cp", "np", "mp", "ns", "ms")}
    kv_stacks = None
    c_stack = None
    for l in range(depth):
        bf = lambda a: a.astype(BF16)
        wi = w_in[l]
        w_conv = bf(wi[:, o_conv:o_mq])
        w_mqkv = bf(wi[:, o_mq:o_mo])
        w_mo = bf(wi[:, o_mo:o_mi])
        w_small = bf(jnp.concatenate(
            [wi[:, o_mi:o_fq], wi[:, o_ff:o_gz],
             jnp.zeros((d, LANES - 2 * nh_m - nh_f), F32)], axis=1))
        w_fox = bf(wi[:, o_fq:o_ff])
        w_gate = bf(wi[:, o_gz:])
        w_br, w_o = bf(w_branch[l]), bf(w_out[l])
        n1, nm, n2 = norm_ffn1[l][None, :], norm_mix[l][None, :], norm_ffn2[l][None, :]
        small_bias = jnp.concatenate(
            [mlstm_b_i[l], mlstm_b_f[l], fox_b_f[l],
             jnp.zeros((LANES - 2 * nh_m - nh_f,), F32)])[None, :]
        m_norm = mlstm_norm[l][None, :]
        qn, kn = fox_q_norm[l][None, :], fox_k_norm[l][None, :]
        cw = conv_w[l]

        xs, hs, g1, u1, d1 = _ffn(xs, n1, ffn1_gate[l], ffn1_up[l], ffn1_down[l], nm, True, True)
        yc_s, u_s = _conv_decode(hs, w_conv, cw, state_conv[l, :, 0, :], state_conv[l, :, 1, :])
        qkv_s = _mm(hs, w_mqkv, BF16, scale=mqkv_scale)
        og_s = _mm(hs, w_mo, BF16, act="sigmoid")
        z_s = _mm(hs, w_small, F32)
        q_s, _, _, kf_s, vf_s = _fox_in(hs, w_fox, qn, kn, dh, 0, 1)
        ym_s, c_stack, n_s, m_s = _mlstm_decode(qkv_s, og_s, z_s, small_bias, m_norm,
                                                state_C, state_n[l], state_m[l], l, c_stack)
        heads = lambda a: a.astype(F32).reshape(n_dec, nh_f, dh)
        yf_s, lf_s = _decode_attn(page_table, heads(q_s), heads(kf_s), heads(vf_s), z_s, small_bias,
                                  cache_k, cache_v, clf, l, gate_col)
        yf_s = yf_s.reshape(n_dec, fox_c).astype(BF16)
        mg_s = _merge(hs, yc_s, ym_s, yf_s, w_gate, w_br)
        xs = _mm(mg_s, w_o, F32, res=xs)
        xs, g2, u2, d2 = _ffn(xs, n2, ffn2_gate[l], ffn2_up[l], ffn2_down[l], n2, False, True)

        xp, hp = _ffn(xp, n1, g1, u1, d1, nm, True)
        yc_p, tail_p = _conv_prompt(hp, w_conv, cw, n_pr)
        qkv_p = _mm(hp, w_mqkv, BF16, scale=mqkv_scale)
        og_p = _mm(hp, w_mo, BF16, act="sigmoid")
        z_p = _mm(hp, w_small, F32)
        q_p, kb_p, vb_p, *kv_stacks = _fox_in(hp, w_fox, qn, kn, dh, l, depth, kv_stacks)
        ym_p, c_p, n_p, m_p = _mlstm_prompt(qkv_p, og_p, z_p, small_bias, m_norm, n_pr, nh_m, dk, dv)
        lf_p, cum_p = _fgate(z_p, small_bias, n_pr)
        f_row = cum_p[:, gate_col:gate_col + nh_f].T
        yf_p = _attn(q_p, kb_p, vb_p, cum_p, f_row, n_pr, nh_f, dh, gate_col)
        mg_p = _merge(hp, yc_p, ym_p, yf_p, w_gate, w_br)
        xp = _mm(mg_p, w_o, F32, res=xp)
        (xp,) = _ffn(xp, n2, g2, u2, d2, n2, False)

        outs["lp"].append(lf_p[:, gate_col:gate_col + nh_f].reshape(n_pr, t_pr, nh_f))
        outs["ks"].append(kf_s.reshape(n_dec, 1, nh_f, dh))
        outs["vs"].append(vf_s.reshape(n_dec, 1, nh_f, dh))
        outs["ls"].append(lf_s[:, gate_col:gate_col + nh_f].reshape(n_dec, 1, nh_f))
        outs["cbp"].append(tail_p)
        outs["cbs"].append(jnp.stack([state_conv[l, :, 1, :], u_s], axis=1))
        outs["cp"].append(c_p)
        outs["np"].append(n_p)
        outs["mp"].append(m_p[:, :nh_m, 0])
        outs["ns"].append(n_s)
        outs["ms"].append(m_s)

    st = {k: jnp.stack(v) for k, v in outs.items()}
    k_prompt, v_prompt = (a.reshape(depth, n_pr, t_pr, nh_f, dh) for a in kv_stacks)
    return (xp.reshape(n_pr, t_pr, d), xs.reshape(n_dec, 1, d),
            k_prompt, v_prompt, st["lp"], st["ks"], st["vs"], st["ls"],
            st["cbp"], st["cbs"], st["cp"], st["np"], st["mp"], c_stack, st["ns"], st["ms"])
```

```python
import functools

import jax
import jax.numpy as jnp
from jax import lax
from jax.experimental import pallas as pl
from jax.experimental.pallas import tpu as pltpu

F32 = jnp.float32
BF16 = jnp.bfloat16
EPS = 1e-6
LANES = 128
VMEM_LIMIT = 56 * 1024 * 1024
M_CHUNK = 512
NT_DIMS = (((1,), (1,)), ((), ()))
LOG2E = 1.4426950408889634


def _params(*sem):
    return pltpu.CompilerParams(dimension_semantics=sem, vmem_limit_bytes=VMEM_LIMIT)


def _blk(n, pref):
    if n <= pref:
        return n
    b = pref
    while n % b:
        b //= 2
    return b


def _rms(x, w):
    return x * lax.rsqrt(jnp.mean(x * x, axis=-1, keepdims=True) + EPS) * w


def _dot(a, b):
    return jnp.dot(a, b, preferred_element_type=F32)


def _log_sigmoid(x):
    return jnp.minimum(x, 0.0) - jnp.log(1.0 + jnp.exp(-jnp.abs(x)))


def _ffn_kernel(x_ref, n_ref, wg_ref, wu_ref, wd_ref, nn_ref, o_ref, *rest, emit_next, emit_bf16):
    rest = list(rest)
    hn_ref = rest.pop(0) if emit_next else None
    wb_refs = [rest.pop(0) for _ in range(3)] if emit_bf16 else None
    (h_scr,) = rest
    j = pl.program_id(1)

    @pl.when(j == 0)
    def _():
        x = x_ref[...]
        h_scr[...] = _rms(x, n_ref[...]).astype(BF16)
        o_ref[...] = x

    wg, wu, wd = wg_ref[...], wu_ref[...], wd_ref[...]
    if emit_bf16:
        wg, wu, wd = wg.astype(BF16), wu.astype(BF16), wd.astype(BF16)
        for ref, val in zip(wb_refs, (wg, wu, wd)):
            ref[...] = val
    h = h_scr[...]
    g = _dot(h, wg)
    u = _dot(h, wu)
    a = (0.5 * g * jax.nn.sigmoid(g) * u).astype(BF16)
    o_ref[...] += _dot(a, wd)

    if emit_next:
        @pl.when(j == pl.num_programs(1) - 1)
        def _():
            hn_ref[...] = _rms(o_ref[...], nn_ref[...]).astype(BF16)


def _ffn(x, norm, wg, wu, wd, next_norm, emit_next, emit_bf16=False, bm_pref=512, bf_pref=512):
    m, d = x.shape
    f = wg.shape[1]
    bm, bf = _blk(m, bm_pref), _blk(f, bf_pref)
    row = pl.BlockSpec((bm, d), lambda i, j: (i, 0))
    w_specs = [pl.BlockSpec((d, bf), lambda i, j: (0, j)),
               pl.BlockSpec((d, bf), lambda i, j: (0, j)),
               pl.BlockSpec((bf, d), lambda i, j: (j, 0))]
    out_shape = [jax.ShapeDtypeStruct((m, d), F32)]
    out_specs = [row]
    if emit_next:
        out_shape.append(jax.ShapeDtypeStruct((m, d), BF16))
        out_specs.append(row)
    if emit_bf16:
        assert m == bm, "each weight block must be visited exactly once"
        out_shape += [jax.ShapeDtypeStruct(w.shape, BF16) for w in (wg, wu, wd)]
        out_specs += w_specs
    return pl.pallas_call(
        functools.partial(_ffn_kernel, emit_next=emit_next, emit_bf16=emit_bf16),
        grid=(m // bm, f // bf),
        in_specs=[row, pl.BlockSpec((1, d), lambda i, j: (0, 0))] + w_specs
                 + [pl.BlockSpec((1, d), lambda i, j: (0, 0))],
        out_specs=out_specs,
        out_shape=out_shape,
        scratch_shapes=[pltpu.VMEM((bm, d), BF16)],
        name="swiglu_block",
        compiler_params=_params("parallel", "arbitrary"),
    )(x, norm, wg, wu, wd, next_norm)


def _wslice_kernel(*refs, off):
    o_ref = refs[-1]
    if off == 0:
        o_ref[...] = refs[0][...].astype(BF16)
        return
    x = jnp.concatenate([refs[0][...], refs[1][...]], axis=1)
    o_ref[...] = pltpu.roll(x, x.shape[1] - off, 1)[:, :o_ref.shape[1]].astype(BF16)


def _wslice(w_all, layer, start, width, br_pref=512):
    d = w_all.shape[1]
    off = start % LANES
    base = start - off
    bn = 1024
    while base % bn or width % bn:
        bn //= 2
    assert bn >= LANES
    br = _blk(d, br_pref)
    args = [w_all]
    in_specs = [pl.BlockSpec((None, br, bn), lambda i, j: (layer, i, base // bn + j))]
    if off:
        args.append(w_all)
        in_specs.append(pl.BlockSpec((None, br, LANES),
                                     lambda i, j: (layer, i, (base + (j + 1) * bn) // LANES)))
    return pl.pallas_call(
        functools.partial(_wslice_kernel, off=off),
        grid=(d // br, width // bn),
        in_specs=in_specs,
        out_specs=pl.BlockSpec((br, bn), lambda i, j: (i, j)),
        out_shape=jax.ShapeDtypeStruct((d, width), BF16),
        name="weight_slice_bf16",
        compiler_params=_params("parallel", "parallel"),
    )(*args)


def _mm_kernel(*refs, act, has_scale, has_res):
    h_ref, w_ref = refs[0], refs[1]
    k = 2
    acc = _dot(h_ref[...], w_ref[...])
    if has_scale:
        acc = acc * refs[k][...]
        k += 1
    if act == "sigmoid":
        acc = jax.nn.sigmoid(acc)
    if has_res:
        acc = refs[k][...] + acc
        k += 1
    o_ref = refs[k]
    o_ref[...] = acc.astype(o_ref.dtype)


def _mm(h, w, out_dtype, act=None, scale=None, res=None, bm_pref=1024, bn_pref=1024):
    m, kd = h.shape
    n = w.shape[1]
    bm, bn = _blk(m, bm_pref), _blk(n, bn_pref)
    args = [h, w]
    in_specs = [pl.BlockSpec((bm, kd), lambda j, i: (i, 0)),
                pl.BlockSpec((kd, bn), lambda j, i: (0, j))]
    if scale is not None:
        args.append(scale)
        in_specs.append(pl.BlockSpec((1, bn), lambda j, i: (0, j)))
    if res is not None:
        args.append(res)
        in_specs.append(pl.BlockSpec((bm, bn), lambda j, i: (i, j)))
    return pl.pallas_call(
        functools.partial(_mm_kernel, act=act, has_scale=scale is not None, has_res=res is not None),
        grid=(n // bn, m // bm),
        in_specs=in_specs,
        out_specs=pl.BlockSpec((bm, bn), lambda j, i: (i, j)),
        out_shape=jax.ShapeDtypeStruct((m, n), out_dtype),
        name="proj_" + (act or "lin") + ("_res" if res is not None else ""),
        compiler_params=_params("parallel", "parallel"),
    )(*args)


def _conv_prompt_kernel(h_ref, wu_ref, wb_ref, wc_ref, cw_ref, y_ref, tail_ref, carry_scr,
                        *, blocks_per_seq):
    i = pl.program_id(1)

    @pl.when(i % blocks_per_seq == 0)
    def _():
        carry_scr[...] = jnp.zeros_like(carry_scr)

    h = h_ref[...]
    u = _dot(h, wc_ref[...]) * _dot(h, wu_ref[...])
    cb = _dot(h, wb_ref[...])
    bm = u.shape[0]
    row = lax.broadcasted_iota(jnp.int32, u.shape, 0)
    c1 = carry_scr[7:8, :]
    c2 = carry_scr[6:7, :]
    u1 = jnp.where(row >= 1, pltpu.roll(u, 1, 0), c1)
    u2 = jnp.where(row >= 2, pltpu.roll(u, 2, 0), jnp.where(row == 0, c2, c1))
    cw = cw_ref[...]
    y = cb * (cw[0:1, :] * u2 + cw[1:2, :] * u1 + cw[2:3, :] * u)
    y_ref[...] = y.astype(BF16)
    carry_scr[...] = u[bm - 8:, :]
    tail_ref[0] = u[bm - 2:, :]


def _conv_prompt(h, w_conv, conv_w, n_seq, bm_pref=1024, bn_pref=512):
    m, kd = h.shape
    c = conv_w.shape[1]
    t = m // n_seq
    bm, bn = _blk(t, bm_pref), _blk(c, bn_pref)
    nb = c // bn
    bps = t // bm
    return pl.pallas_call(
        functools.partial(_conv_prompt_kernel, blocks_per_seq=bps),
        grid=(nb, m // bm),
        in_specs=[
            pl.BlockSpec((bm, kd), lambda j, i: (i, 0)),
            pl.BlockSpec((kd, bn), lambda j, i: (0, j)),
            pl.BlockSpec((kd, bn), lambda j, i: (0, j + nb)),
            pl.BlockSpec((kd, bn), lambda j, i: (0, j + 2 * nb)),
            pl.BlockSpec((conv_w.shape[0], bn), lambda j, i: (0, j)),
        ],
        out_specs=[
            pl.BlockSpec((bm, bn), lambda j, i: (i, j)),
            pl.BlockSpec((1, 2, bn), lambda j, i: (i // bps, 0, j)),
        ],
        out_shape=[jax.ShapeDtypeStruct((m, c), BF16),
                   jax.ShapeDtypeStruct((n_seq, 2, c), F32)],
        scratch_shapes=[pltpu.VMEM((8, bn), F32)],
        name="conv_prompt",
        compiler_params=_params("parallel", "arbitrary"),
    )(h, w_conv, w_conv, w_conv, conv_w)


def _conv_decode_kernel(h_ref, wu_ref, wb_ref, wc_ref, cw_ref, b0_ref, b1_ref, y_ref, u_ref):
    h = h_ref[...]
    u = _dot(h, wc_ref[...]) * _dot(h, wu_ref[...])
    cb = _dot(h, wb_ref[...])
    cw = cw_ref[...]
    y = cb * (cw[0:1, :] * b0_ref[...] + cw[1:2, :] * b1_ref[...] + cw[2:3, :] * u)
    y_ref[...] = y.astype(BF16)
    u_ref[...] = u


def _conv_decode(h, w_conv, conv_w, buf0, buf1, bn_pref=512):
    m, kd = h.shape
    c = conv_w.shape[1]
    bn = _blk(c, bn_pref)
    nb = c // bn
    return pl.pallas_call(
        _conv_decode_kernel,
        grid=(nb,),
        in_specs=[
            pl.BlockSpec((m, kd), lambda j: (0, 0)),
            pl.BlockSpec((kd, bn), lambda j: (0, j)),
            pl.BlockSpec((kd, bn), lambda j: (0, j + nb)),
            pl.BlockSpec((kd, bn), lambda j: (0, j + 2 * nb)),
            pl.BlockSpec((conv_w.shape[0], bn), lambda j: (0, j)),
            pl.BlockSpec((m, bn), lambda j: (0, j)),
            pl.BlockSpec((m, bn), lambda j: (0, j)),
        ],
        out_specs=[pl.BlockSpec((m, bn), lambda j: (0, j)),
                   pl.BlockSpec((m, bn), lambda j: (0, j))],
        out_shape=[jax.ShapeDtypeStruct((m, c), BF16), jax.ShapeDtypeStruct((m, c), F32)],
        name="conv_decode",
        compiler_params=_params("parallel"),
    )(h, w_conv, w_conv, w_conv, conv_w, buf0, buf1)


def _fox_in_kernel(h_ref, wq_ref, wk_ref, wv_ref, qn_ref, kn_ref, *rest, dh, q_scale):
    q_ref, kb_ref, vb_ref, kf_ref, vf_ref = rest[-5:]
    h = h_ref[...]
    fq = _dot(h, wq_ref[...])
    fk = _dot(h, wk_ref[...])
    fv = _dot(h, wv_ref[...])
    for hh in range(fq.shape[1] // dh):
        sl = slice(hh * dh, (hh + 1) * dh)
        qh = _rms(fq[:, sl], qn_ref[...])
        kh = _rms(fk[:, sl], kn_ref[...])
        q_ref[:, sl] = (qh * q_scale).astype(BF16)
        kf_ref[:, sl] = kh
        kb_ref[:, sl] = kh.astype(BF16)
    vf_ref[...] = fv
    vb_ref[...] = fv.astype(BF16)


def _fox_in(h, w_fox, qn, kn, dh, layer, depth, stacks=None, bm_pref=1024, bn_pref=512):
    m, kd = h.shape
    c = w_fox.shape[1] // 3
    bm, bn = _blk(m, bm_pref), _blk(c, bn_pref)
    nb = c // bn
    o_spec = pl.BlockSpec((bm, bn), lambda j, i: (i, j))
    s_spec = pl.BlockSpec((None, bm, bn), lambda j, i: (layer, i, j))
    args = [h, w_fox, w_fox, w_fox, qn, kn]
    in_specs = [
        pl.BlockSpec((bm, kd), lambda j, i: (i, 0)),
        pl.BlockSpec((kd, bn), lambda j, i: (0, j)),
        pl.BlockSpec((kd, bn), lambda j, i: (0, j + nb)),
        pl.BlockSpec((kd, bn), lambda j, i: (0, j + 2 * nb)),
        pl.BlockSpec((1, dh), lambda j, i: (0, 0)),
        pl.BlockSpec((1, dh), lambda j, i: (0, 0)),
    ]
    aliases = {}
    if stacks is not None:
        aliases = {len(args): 3, len(args) + 1: 4}
        args += list(stacks)
        in_specs += [pl.BlockSpec(memory_space=pl.ANY)] * 2
    return pl.pallas_call(
        functools.partial(_fox_in_kernel, dh=dh, q_scale=dh ** -0.5 * LOG2E),
        grid=(nb, m // bm),
        in_specs=in_specs,
        out_specs=[o_spec] * 3 + [s_spec] * 2,
        out_shape=[jax.ShapeDtypeStruct((m, c), BF16)] * 3
                  + [jax.ShapeDtypeStruct((depth, m, c), F32)] * 2,
        input_output_aliases=aliases,
        name="fox_qkv",
        compiler_params=_params("parallel", "parallel"),
    )(*args)


def _mlstm_prompt_kernel(qkv_ref, og_ref, z_ref, b_ref, mn_ref,
                         y_ref, c_ref, n_ref, m_ref, *, nh, dk, dv):
    @pl.when(pl.program_id(1) == 0)
    def _():
        c_ref[...] = jnp.zeros_like(c_ref)
        n_ref[...] = jnp.zeros_like(n_ref)
        m_ref[...] = jnp.zeros_like(m_ref)

    z = z_ref[...] + b_ref[...]
    L = z.shape[0]
    row = lax.broadcasted_iota(jnp.int32, z.shape, 0)
    lane = lax.broadcasted_iota(jnp.int32, z.shape, 1)
    bt = _log_sigmoid(z)
    s = 1
    while s < L:
        bt = bt + jnp.where(row >= s, pltpu.roll(bt, s, 0), 0.0)
        s *= 2
    g = jnp.where(lane < nh, z, bt)
    g_t = g.T
    causal = (lax.broadcasted_iota(jnp.int32, (L, L), 0)
              >= lax.broadcasted_iota(jnp.int32, (L, L), 1))
    for h in range(nh):
        q = qkv_ref[:, h * dk:(h + 1) * dk]
        k = qkv_ref[:, nh * dk + h * dk: nh * dk + (h + 1) * dk]
        v = qkv_ref[:, 2 * nh * dk + h * dv: 2 * nh * dk + (h + 1) * dv]
        ig_col, bt_col = g[:, h:h + 1], g[:, nh + h:nh + h + 1]
        ig_row, bt_row = g_t[h:h + 1, :], g_t[nh + h:nh + h + 1, :]
        m_prev = m_ref[0, h:h + 1, 0:1]
        dmat = jnp.where(causal, bt_col - bt_row + ig_row, -jnp.inf)
        inter = bt_col + m_prev
        m_t = jnp.maximum(inter, jnp.max(dmat, axis=1, keepdims=True))
        w = jnp.exp(dmat - m_t)
        a = jnp.exp(inter - m_t)
        sc = lax.dot_general(q, k, NT_DIMS, preferred_element_type=F32) * w
        c_old = c_ref[0, h]
        n_old = n_ref[0, h:h + 1, :]
        num = a * _dot(q, c_old.astype(BF16)) + _dot(sc.astype(BF16), v)
        den = (a * jnp.sum(q.astype(F32) * n_old, axis=1, keepdims=True)
               + jnp.sum(sc, axis=1, keepdims=True))
        hm = num / jnp.maximum(jnp.abs(den), jnp.exp(-m_t))
        m_last = m_t[L - 1:L, :]
        a_last = a[L - 1:L, :]
        w_last = jnp.exp(bt_col[L - 1:L, :] - bt_col + ig_col - m_last)
        kw = k.astype(F32) * w_last
        c_ref[0, h] = a_last * c_old + _dot(kw.T.astype(BF16), v)
        n_ref[0, h:h + 1, :] = a_last * n_old + jnp.sum(kw, axis=0, keepdims=True)
        m_ref[0, h:h + 1, :] = jnp.broadcast_to(m_last, (1, m_ref.shape[2]))
        vs = slice(h * dv, (h + 1) * dv)
        y_ref[:, vs] = (og_ref[:, vs].astype(F32) * _rms(hm, mn_ref[:, vs])).astype(BF16)


def _mlstm_prompt(qkv, og, z, bias, m_norm, n_seq, nh, dk, dv):
    m = qkv.shape[0]
    t = m // n_seq
    L = M_CHUNK if t % M_CHUNK == 0 else t
    nc = t // L
    bw = nh * dv
    return pl.pallas_call(
        functools.partial(_mlstm_prompt_kernel, nh=nh, dk=dk, dv=dv),
        grid=(n_seq, nc),
        in_specs=[
            pl.BlockSpec((L, qkv.shape[1]), lambda n, c: (n * nc + c, 0)),
            pl.BlockSpec((L, bw), lambda n, c: (n * nc + c, 0)),
            pl.BlockSpec((L, LANES), lambda n, c: (n * nc + c, 0)),
            pl.BlockSpec((1, LANES), lambda n, c: (0, 0)),
            pl.BlockSpec((1, bw), lambda n, c: (0, 0)),
        ],
        out_specs=[
            pl.BlockSpec((L, bw), lambda n, c: (n * nc + c, 0)),
            pl.BlockSpec((1, nh, dk, dv), lambda n, c: (n, 0, 0, 0)),
            pl.BlockSpec((1, nh, dk), lambda n, c: (n, 0, 0)),
            pl.BlockSpec((1, 8, LANES), lambda n, c: (n, 0, 0)),
        ],
        out_shape=[
            jax.ShapeDtypeStruct((m, bw), BF16),
            jax.ShapeDtypeStruct((n_seq, nh, dk, dv), F32),
            jax.ShapeDtypeStruct((n_seq, nh, dk), F32),
            jax.ShapeDtypeStruct((n_seq, 8, LANES), F32),
        ],
        name="mlstm_prompt",
        compiler_params=_params("parallel", "arbitrary"),
    )(qkv, og, z, bias, m_norm)


def _mlstm_decode_kernel(qkv_ref, og_ref, z_ref, b_ref, mn_ref, c_ref, n_ref, m_ref,
                         *rest, nh, dk, dv):
    y_ref, co_ref, no_ref, mo_ref = rest[-4:]
    bs = qkv_ref.shape[0]
    qkv = qkv_ref[...].astype(F32)
    og = og_ref[...].astype(F32)
    z = z_ref[...] + b_ref[...]
    lf_all = _log_sigmoid(z)
    eye = (lax.broadcasted_iota(jnp.int32, (dk, dk), 0)
           == lax.broadcasted_iota(jnp.int32, (dk, dk), 1))
    for s in range(bs):
        for h in range(nh):
            q = qkv[s:s + 1, h * dk:(h + 1) * dk]
            k = qkv[s:s + 1, nh * dk + h * dk: nh * dk + (h + 1) * dk]
            v = qkv[s:s + 1, 2 * nh * dk + h * dv: 2 * nh * dk + (h + 1) * dv]
            ig = z[s:s + 1, h:h + 1]
            lf = lf_all[s:s + 1, nh + h:nh + h + 1]
            m_old = m_ref[s:s + 1, h:h + 1]
            c_old = c_ref[s, h]
            n_old = n_ref[s, h:h + 1, :]
            inter = lf + m_old
            m_t = jnp.maximum(inter, ig)
            w = jnp.exp(ig - m_t)
            a = jnp.exp(inter - m_t)
            sc = jnp.sum(q * k, axis=1, keepdims=True) * w
            num = a * _dot(q.astype(BF16), c_old.astype(BF16)) + sc * v
            den = a * jnp.sum(q * n_old, axis=1, keepdims=True) + sc
            hm = num / jnp.maximum(jnp.abs(den), jnp.exp(-m_t))
            k_col = jnp.sum(jnp.where(eye, k, 0.0), axis=1, keepdims=True)
            co_ref[s, h] = a * c_old + (w * k_col) * v
            no_ref[s, h:h + 1, :] = a * n_old + w * k
            mo_ref[s:s + 1, h:h + 1] = m_t
            vs = slice(h * dv, (h + 1) * dv)
            y_ref[s:s + 1, vs] = (og[s:s + 1, vs] * _rms(hm, mn_ref[:, vs])).astype(BF16)


def _mlstm_decode(qkv, og, z, bias, m_norm, c_all, n, m_state, layer, c_stack=None):
    nd = qkv.shape[0]
    nh, dk, dv = c_all.shape[2:]
    bs = _blk(nd, 8)
    bw = nh * dv
    row = lambda i: (i, 0)
    c_spec = pl.BlockSpec((None, bs, nh, dk, dv), lambda i: (layer, i, 0, 0, 0))
    args = [qkv, og, z, bias, m_norm, c_all, n, m_state]
    in_specs = [
        pl.BlockSpec((bs, qkv.shape[1]), row),
        pl.BlockSpec((bs, bw), row),
        pl.BlockSpec((bs, LANES), row),
        pl.BlockSpec((1, LANES), lambda i: (0, 0)),
        pl.BlockSpec((1, bw), lambda i: (0, 0)),
        c_spec,
        pl.BlockSpec((bs, nh, dk), lambda i: (i, 0, 0)),
        pl.BlockSpec((bs, nh), row),
    ]
    aliases = {}
    if c_stack is not None:
        aliases = {len(args): 1}
        args.append(c_stack)
        in_specs.append(pl.BlockSpec(memory_space=pl.ANY))
    return pl.pallas_call(
        functools.partial(_mlstm_decode_kernel, nh=nh, dk=dk, dv=dv),
        grid=(nd // bs,),
        in_specs=in_specs,
        out_specs=[
            pl.BlockSpec((bs, bw), row),
            c_spec,
            pl.BlockSpec((bs, nh, dk), lambda i: (i, 0, 0)),
            pl.BlockSpec((bs, nh), row),
        ],
        out_shape=[
            jax.ShapeDtypeStruct((nd, bw), BF16),
            jax.ShapeDtypeStruct(c_all.shape, F32),
            jax.ShapeDtypeStruct(n.shape, F32),
            jax.ShapeDtypeStruct(m_state.shape, F32),
        ],
        input_output_aliases=aliases,
        name="mlstm_decode",
        compiler_params=_params("parallel"),
    )(*args)


def _split3(x):
    hi = x.astype(BF16).astype(F32)
    mid = (x - hi).astype(BF16).astype(F32)
    lo = (x - hi - mid).astype(BF16).astype(F32)
    return hi, mid, lo


def _fgate_kernel(z_ref, b_ref, lf_ref, qa_ref, ka_ref, *, nh, gate_col):
    lf = _log_sigmoid(z_ref[...] + b_ref[...])
    lf_ref[...] = lf
    t = lf.shape[0]
    row = lax.broadcasted_iota(jnp.int32, lf.shape, 0)
    lane = lax.broadcasted_iota(jnp.int32, lf.shape, 1)
    acc = lf
    s = 1
    while s < t:
        acc = acc + jnp.where(row >= s, pltpu.roll(acc, s, 0), 0.0)
        s *= 2
    acc = acc * LOG2E
    for h in range(nh):
        hi, mid, lo = _split3(acc[:, gate_col + h:gate_col + h + 1])
        terms = jnp.where(lane % 3 == 0, hi, jnp.where(lane % 3 == 1, mid, lo))
        sl = slice(h * LANES, (h + 1) * LANES)
        qa_ref[:, sl] = jnp.where(lane < 3, terms, jnp.where(lane < 6, 1.0, 0.0)).astype(BF16)
        ka_ref[:, sl] = jnp.where(lane < 3, 1.0, jnp.where(lane < 6, -terms, 0.0)).astype(BF16)


def _fgate(z, bias, n_seq, nh, gate_col):
    m = z.shape[0]
    t = m // n_seq
    spec = pl.BlockSpec((t, LANES), lambda n: (n, 0))
    aug = pl.BlockSpec((t, nh * LANES), lambda n: (n, 0))
    return pl.pallas_call(
        functools.partial(_fgate_kernel, nh=nh, gate_col=gate_col),
        grid=(n_seq,),
        in_specs=[spec, pl.BlockSpec((1, LANES), lambda n: (0, 0))],
        out_specs=[spec, aug, aug],
        out_shape=[jax.ShapeDtypeStruct((m, LANES), F32)]
                  + [jax.ShapeDtypeStruct((m, nh * LANES), BF16)] * 2,
        name="fox_gate_cumsum",
        compiler_params=_params("parallel"),
    )(z, bias)


def _attn_kernel(q_ref, qa_ref, k_ref, ka_ref, v_ref, o_ref, m_scr, l_scr, acc_scr, *, nh, dh):
    i, j = pl.program_id(1), pl.program_id(2)
    bq, bk = q_ref.shape[0], k_ref.shape[0]
    nrep = bk // LANES

    @pl.when(j == 0)
    def _():
        m_scr[...] = jnp.full_like(m_scr, -jnp.inf)
        l_scr[...] = jnp.zeros_like(l_scr)
        acc_scr[...] = jnp.zeros_like(acc_scr)

    def update(on_diagonal):
        if on_diagonal:
            mask = (lax.broadcasted_iota(jnp.int32, (bq, bk), 0)
                    >= lax.broadcasted_iota(jnp.int32, (bq, bk), 1))
        ones = jnp.ones((bk, LANES), BF16)
        for h in range(nh):
            sl = slice(h * dh, (h + 1) * dh)
            s = lax.dot_general(jnp.concatenate([q_ref[:, sl], qa_ref[:, sl]], axis=1),
                                jnp.concatenate([k_ref[:, sl], ka_ref[:, sl]], axis=1),
                                NT_DIMS, preferred_element_type=F32)
            if on_diagonal:
                s = jnp.where(mask, s, -jnp.inf)
            m_prev = m_scr[h]
            m_new = jnp.maximum(m_prev, jnp.max(s, axis=1, keepdims=True))
            p = jnp.exp2(s - jnp.concatenate([m_new] * nrep, axis=1))
            alpha = jnp.exp2(m_prev - m_new)
            pv = _dot(p.astype(BF16), jnp.concatenate([v_ref[:, sl], ones], axis=1))
            l_scr[h] = alpha * l_scr[h] + pv[:, dh:]
            acc_scr[:, sl] = alpha * acc_scr[:, sl] + pv[:, :dh]
            m_scr[h] = m_new

    @pl.when(j < i)
    def _():
        update(False)

    @pl.when(j == i)
    def _():
        update(True)

    @pl.when(j == pl.num_programs(2) - 1)
    def _():
        for h in range(nh):
            sl = slice(h * dh, (h + 1) * dh)
            o_ref[:, sl] = (acc_scr[:, sl] / l_scr[h]).astype(BF16)


def _attn(q, qa, k, ka, v, n_seq, nh, dh, blk_pref=512):
    m, c = q.shape
    t = m // n_seq
    assert dh == LANES, "running softmax stats and the bias columns are one head-dim wide"
    bq = bk = _blk(t, blk_pref)
    nq = nk = t // bq
    q_spec = pl.BlockSpec((bq, c), lambda b, i, j: (b * nq + i, 0))
    kv_spec = pl.BlockSpec((bk, c), lambda b, i, j: (b * nk + jnp.minimum(j, i), 0))
    return pl.pallas_call(
        functools.partial(_attn_kernel, nh=nh, dh=dh),
        grid=(n_seq, nq, nk),
        in_specs=[q_spec, q_spec, kv_spec, kv_spec, kv_spec],
        out_specs=q_spec,
        out_shape=jax.ShapeDtypeStruct((m, c), BF16),
        scratch_shapes=[pltpu.VMEM((nh, bq, LANES), F32), pltpu.VMEM((nh, bq, LANES), F32),
                        pltpu.VMEM((bq, c), F32)],
        name="fox_prompt_attn",
        compiler_params=_params("parallel", "parallel", "arbitrary"),
    )(q, qa, k, ka, v)


def _decode_attn_kernel(pt_ref, q_ref, kn_ref, vn_ref, z_ref, b_ref, *rest,
                        nh, dh, gate_col, pps):
    k_refs, v_refs, lf_refs = rest[:pps], rest[pps:2 * pps], rest[2 * pps:3 * pps]
    o_ref, lfn_ref, lf_scr, m_scr, l_scr, acc_scr, carry_scr = rest[3 * pps:]
    step = pl.program_id(1)
    page = k_refs[0].shape[0]
    w = page * nh
    q = q_ref[0]
    qb = q.astype(BF16)
    lf_new = _log_sigmoid(z_ref[0] + b_ref[...])
    lfn_ref[0] = lf_new
    pick = (lax.broadcasted_iota(jnp.int32, (nh, LANES), 1)
            == lax.broadcasted_iota(jnp.int32, (nh, LANES), 0) + gate_col)
    fn_col = jnp.sum(jnp.where(pick, jnp.broadcast_to(lf_new, (nh, LANES)), 0.0),
                     axis=1, keepdims=True) * LOG2E
    own_head = (lax.broadcasted_iota(jnp.int32, (nh, w), 1) % nh
                == lax.broadcasted_iota(jnp.int32, (nh, w), 0))

    @pl.when(step == 0)
    def _():
        m_scr[...] = jnp.full_like(m_scr, -jnp.inf)
        l_scr[...] = jnp.zeros_like(l_scr)
        acc_scr[...] = jnp.zeros_like(acc_scr)
        carry_scr[...] = jnp.zeros_like(carry_scr)

    for r in range(pps):
        lf_scr[r:r + 1, :] = lf_refs[r][...]
    lf = lf_scr[...]
    pos = lax.broadcasted_iota(jnp.int32, lf.shape, 1)
    rowi = lax.broadcasted_iota(jnp.int32, lf.shape, 0)
    suf, tot = lf, lf
    sh = nh
    while sh < w:
        suf = suf + jnp.where(pos < w - sh, pltpu.roll(suf, w - sh, 1), 0.0)
        tot = tot + pltpu.roll(tot, sh, 1)
        sh *= 2
    newer = tot
    sh = 1
    while sh < pps:
        newer = newer + jnp.where(rowi >= sh, pltpu.roll(newer, sh, 0), 0.0)
        sh *= 2
    bias = (suf - lf + newer - tot + carry_scr[...]) * LOG2E

    scores = []
    for r in range(pps):
        k2 = k_refs[r][...].reshape(w, dh).astype(BF16)
        s = lax.dot_general(qb, k2, NT_DIMS, preferred_element_type=F32)
        scores.append(jnp.where(own_head, s + fn_col + bias[r:r + 1, :], -jnp.inf))
    m_prev = m_scr[...]
    m_new = jnp.maximum(m_prev, jnp.max(functools.reduce(jnp.maximum, scores),
                                        axis=1, keepdims=True))
    alpha = jnp.exp2(m_prev - m_new)
    l_run = alpha * l_scr[...]
    acc = alpha * acc_scr[...]
    for r in range(pps):
        pe = jnp.exp2(scores[r] - m_new)
        l_run = l_run + jnp.sum(pe, axis=1, keepdims=True)
        acc = acc + _dot(pe.astype(BF16), v_refs[r][...].reshape(w, dh).astype(BF16))
    m_scr[...] = m_new
    l_scr[...] = l_run
    acc_scr[...] = acc
    carry_scr[...] = carry_scr[...] + newer[pps - 1:pps, :]

    @pl.when(step == pl.num_programs(1) - 1)
    def _():
        s_new = jnp.sum(q * kn_ref[0], axis=1, keepdims=True)
        m_fin = jnp.maximum(m_new, s_new)
        al = jnp.exp2(m_new - m_fin)
        pn = jnp.exp2(s_new - m_fin)
        o_ref[0] = ((al * acc + pn * vn_ref[0]) / (al * l_run + pn)).astype(o_ref.dtype)


def _decode_attn(page_table, q, k_new, v_new, z, bias, cache_k, cache_v, cache_lf,
                 layer, gate_col, pps_pref=16):
    nd, nh, dh = q.shape
    n_pages = page_table.shape[1]
    page = cache_k.shape[2]
    w = page * nh
    assert (page & (page - 1)) == 0, "per-head scans over a page use power-of-two strides"
    pps = _blk(n_pages, pps_pref)
    assert (pps & (pps - 1)) == 0
    row3 = lambda n, p, pt: (n, 0, 0)

    def page_map(r, tail):
        return lambda n, p, pt: (layer, pt[n, n_pages - 1 - (p * pps + r)]) + tail

    kv_specs = [pl.BlockSpec((None, None, page, nh, dh), page_map(r, (0, 0, 0))) for r in range(pps)]
    lf_specs = [pl.BlockSpec((None, None, 1, w), page_map(r, (0, 0))) for r in range(pps)]
    grid_spec = pltpu.PrefetchScalarGridSpec(
        num_scalar_prefetch=1,
        grid=(nd, n_pages // pps),
        in_specs=[
            pl.BlockSpec((1, nh, dh), row3),
            pl.BlockSpec((1, nh, dh), row3),
            pl.BlockSpec((1, nh, dh), row3),
            pl.BlockSpec((1, 1, LANES), row3),
            pl.BlockSpec((1, LANES), lambda n, p, pt: (0, 0)),
        ] + kv_specs + kv_specs + lf_specs,
        out_specs=[pl.BlockSpec((1, nh, dh), row3), pl.BlockSpec((1, 1, LANES), row3)],
        scratch_shapes=[pltpu.VMEM((pps, w), F32), pltpu.VMEM((nh, 1), F32), pltpu.VMEM((nh, 1), F32),
                        pltpu.VMEM((nh, dh), F32), pltpu.VMEM((1, w), F32)],
    )
    out, lf_new = pl.pallas_call(
        functools.partial(_decode_attn_kernel, nh=nh, dh=dh, gate_col=gate_col, pps=pps),
        grid_spec=grid_spec,
        out_shape=[jax.ShapeDtypeStruct((nd, nh, dh), F32),
                   jax.ShapeDtypeStruct((nd, 1, LANES), F32)],
        name="fox_decode_attn",
        compiler_params=_params("parallel", "arbitrary"),
    )(page_table, q, k_new, v_new, z.reshape(nd, 1, LANES), bias,
      *([cache_k] * pps), *([cache_v] * pps), *([cache_lf] * pps))
    return out, lf_new.reshape(nd, LANES)


def _merge_kernel(h_ref, yc_ref, ym_ref, yf_ref, wg_ref, wb_ref, o_ref, acc_scr):
    b = pl.program_id(2)
    gate = jax.nn.sigmoid(_dot(h_ref[...], wg_ref[...]))

    @pl.when(b == 0)
    def _():
        acc_scr[...] = gate * _dot(yc_ref[...], wb_ref[...])

    @pl.when(b == 1)
    def _():
        acc_scr[...] += gate * _dot(ym_ref[...], wb_ref[...])

    @pl.when(b == 2)
    def _():
        o_ref[...] = (acc_scr[...] + gate * _dot(yf_ref[...], wb_ref[...])).astype(BF16)


def _merge(h, yc, ym, yf, w_gate, w_branch, bm_pref=1024, bn_pref=1024):
    m, d = h.shape
    bw = yc.shape[1]
    bm, bn = _blk(m, bm_pref), _blk(d, bn_pref)
    nc = d // bn
    y_spec = pl.BlockSpec((bm, bw), lambda i, c, b: (i, 0))
    return pl.pallas_call(
        _merge_kernel,
        grid=(m // bm, nc, 3),
        in_specs=[
            pl.BlockSpec((bm, d), lambda i, c, b: (i, 0)),
            y_spec, y_spec, y_spec,
            pl.BlockSpec((d, bn), lambda i, c, b: (0, b * nc + c)),
            pl.BlockSpec((None, bw, bn), lambda i, c, b: (b, 0, c)),
        ],
        out_specs=pl.BlockSpec((bm, bn), lambda i, c, b: (i, c)),
        out_shape=jax.ShapeDtypeStruct((m, d), BF16),
        scratch_shapes=[pltpu.VMEM((bm, bn), F32)],
        name="gated_merge",
        compiler_params=_params("parallel", "parallel", "arbitrary"),
    )(h, yc, ym, yf, w_gate, w_branch)


def kernel(x_prompt, x_sample, cache_k, cache_v, cache_logf, page_table, state_conv, state_C, state_n, state_m, norm_ffn1, ffn1_gate, ffn1_up, ffn1_down, norm_mix, w_in, conv_w, mlstm_b_i, mlstm_b_f, mlstm_norm, fox_b_f, fox_q_norm, fox_k_norm, w_branch, w_out, norm_ffn2, ffn2_gate, ffn2_up, ffn2_down):
    depth = w_in.shape[0]
    n_pr, t_pr, d = x_prompt.shape
    n_dec = x_sample.shape[0]
    conv_c = conv_w.shape[2]
    nh_m, dk, dv = state_C.shape[2:]
    n_pool, page, nh_f, dh = cache_k.shape[1:]
    fox_c = nh_f * dh
    assert x_sample.shape[1] == 1, "sample group is one new token per sequence"
    assert 2 * nh_m + nh_f <= LANES

    o_conv = 0
    o_mq = 3 * conv_c
    o_mo = o_mq + 2 * nh_m * dk + nh_m * dv
    o_mi = o_mo + nh_m * dv
    o_fq = o_mi + 2 * nh_m
    o_ff = o_fq + 3 * fox_c
    o_gz = o_ff + nh_f
    gate_col = 2 * nh_m

    xp = x_prompt.reshape(n_pr * t_pr, d)
    xs = x_sample.reshape(n_dec, d)
    clf = cache_logf.reshape(depth, n_pool, 1, page * nh_f)

    mqkv_scale = jnp.concatenate([jnp.ones((nh_m * dk,), F32),
                                  jnp.full((nh_m * dk,), dk ** -0.5, F32),
                                  jnp.ones((nh_m * dv,), F32)])[None, :]
    outs = {k: [] for k in ("lp", "ks", "vs", "ls", "cbp", "cbs", "cp", "np", "mp", "ns", "ms")}
    kv_stacks = None
    c_stack = None
    for l in range(depth):
        bf = lambda a: a.astype(BF16)
        wi = w_in[l]
        w_conv = _wslice(w_in, l, o_conv, o_mq - o_conv)
        w_mqkv = _wslice(w_in, l, o_mq, o_mo - o_mq)
        w_mo = _wslice(w_in, l, o_mo, o_mi - o_mo)
        w_small = bf(jnp.concatenate(
            [wi[:, o_mi:o_fq], wi[:, o_ff:o_gz],
             jnp.zeros((d, LANES - 2 * nh_m - nh_f), F32)], axis=1))
        w_fox = _wslice(w_in, l, o_fq, o_ff - o_fq)
        w_gate = _wslice(w_in, l, o_gz, w_in.shape[2] - o_gz)
        w_br, w_o = bf(w_branch[l]), bf(w_out[l])
        n1, nm, n2 = norm_ffn1[l][None, :], norm_mix[l][None, :], norm_ffn2[l][None, :]
        small_bias = jnp.concatenate(
            [mlstm_b_i[l], mlstm_b_f[l], fox_b_f[l],
             jnp.zeros((LANES - 2 * nh_m - nh_f,), F32)])[None, :]
        m_norm = mlstm_norm[l][None, :]
        qn, kn = fox_q_norm[l][None, :], fox_k_norm[l][None, :]
        cw = conv_w[l]

        xs, hs, g1, u1, d1 = _ffn(xs, n1, ffn1_gate[l], ffn1_up[l], ffn1_down[l], nm, True, True)
        yc_s, u_s = _conv_decode(hs, w_conv, cw, state_conv[l, :, 0, :], state_conv[l, :, 1, :])
        qkv_s = _mm(hs, w_mqkv, BF16, scale=mqkv_scale)
        og_s = _mm(hs, w_mo, BF16, act="sigmoid")
        z_s = _mm(hs, w_small, F32)
        q_s, _, _, kf_s, vf_s = _fox_in(hs, w_fox, qn, kn, dh, 0, 1)
        ym_s, c_stack, n_s, m_s = _mlstm_decode(qkv_s, og_s, z_s, small_bias, m_norm,
                                                state_C, state_n[l], state_m[l], l, c_stack)
        heads = lambda a: a.astype(F32).reshape(n_dec, nh_f, dh)
        yf_s, lf_s = _decode_attn(page_table, heads(q_s), heads(kf_s), heads(vf_s), z_s, small_bias,
                                  cache_k, cache_v, clf, l, gate_col)
        yf_s = yf_s.reshape(n_dec, fox_c).astype(BF16)
        mg_s = _merge(hs, yc_s, ym_s, yf_s, w_gate, w_br)
        xs = _mm(mg_s, w_o, F32, res=xs)
        xs, g2, u2, d2 = _ffn(xs, n2, ffn2_gate[l], ffn2_up[l], ffn2_down[l], n2, False, True)

        xp, hp = _ffn(xp, n1, g1, u1, d1, nm, True)
        yc_p, tail_p = _conv_prompt(hp, w_conv, cw, n_pr)
        qkv_p = _mm(hp, w_mqkv, BF16, scale=mqkv_scale)
        og_p = _mm(hp, w_mo, BF16, act="sigmoid")
        z_p = _mm(hp, w_small, F32)
        q_p, kb_p, vb_p, *kv_stacks = _fox_in(hp, w_fox, qn, kn, dh, l, depth, kv_stacks)
        ym_p, c_p, n_p, m_p = _mlstm_prompt(qkv_p, og_p, z_p, small_bias, m_norm, n_pr, nh_m, dk, dv)
        lf_p, qa_p, ka_p = _fgate(z_p, small_bias, n_pr, nh_f, gate_col)
        yf_p = _attn(q_p, qa_p, kb_p, ka_p, vb_p, n_pr, nh_f, dh)
        mg_p = _merge(hp, yc_p, ym_p, yf_p, w_gate, w_br)
        xp = _mm(mg_p, w_o, F32, res=xp)
        (xp,) = _ffn(xp, n2, g2, u2, d2, n2, False)

        outs["lp"].append(lf_p[:, gate_col:gate_col + nh_f].reshape(n_pr, t_pr, nh_f))
        outs["ks"].append(kf_s.reshape(n_dec, 1, nh_f, dh))
        outs["vs"].append(vf_s.reshape(n_dec, 1, nh_f, dh))
        outs["ls"].append(lf_s[:, gate_col:gate_col + nh_f].reshape(n_dec, 1, nh_f))
        outs["cbp"].append(tail_p)
        outs["cbs"].append(jnp.stack([state_conv[l, :, 1, :], u_s], axis=1))
        outs["cp"].append(c_p)
        outs["np"].append(n_p)
        outs["mp"].append(m_p[:, :nh_m, 0])
        outs["ns"].append(n_s)
        outs["ms"].append(m_s)

    st = {k: jnp.stack(v) for k, v in outs.items()}
    k_prompt, v_prompt = (a.reshape(depth, n_pr, t_pr, nh_f, dh) for a in kv_stacks)
    return (xp.reshape(n_pr, t_pr, d), xs.reshape(n_dec, 1, d),
            k_prompt, v_prompt, st["lp"], st["ks"], st["vs"], st["ls"],
            st["cbp"], st["cbs"], st["cp"], st["np"], st["mp"], c_stack, st["ns"], st["ms"])
```

```python
import functools

import jax
import jax.numpy as jnp
from jax import lax
from jax.experimental import pallas as pl
from jax.experimental.pallas import tpu as pltpu

F32 = jnp.float32
BF16 = jnp.bfloat16
EPS = 1e-6
LANES = 128
VMEM_LIMIT = 56 * 1024 * 1024
M_CHUNK = 512
NT_DIMS = (((1,), (1,)), ((), ()))
LOG2E = 1.4426950408889634


def _params(*sem):
    return pltpu.CompilerParams(dimension_semantics=sem, vmem_limit_bytes=VMEM_LIMIT)


def _blk(n, pref):
    if n <= pref:
        return n
    b = pref
    while n % b:
        b //= 2
    return b


def _rms(x, w):
    return x * lax.rsqrt(jnp.mean(x * x, axis=-1, keepdims=True) + EPS) * w


def _dot(a, b):
    return jnp.dot(a, b, preferred_element_type=F32)


def _log_sigmoid(x):
    return jnp.minimum(x, 0.0) - jnp.log(1.0 + jnp.exp(-jnp.abs(x)))


def _ffn_kernel(x_ref, n_ref, wg_ref, wu_ref, wd_ref, nn_ref, o_ref, *rest, emit_next, emit_bf16):
    rest = list(rest)
    hn_ref = rest.pop(0) if emit_next else None
    wb_refs = [rest.pop(0) for _ in range(3)] if emit_bf16 else None
    (h_scr,) = rest
    j = pl.program_id(1)

    @pl.when(j == 0)
    def _():
        x = x_ref[...]
        h_scr[...] = _rms(x, n_ref[...]).astype(BF16)
        o_ref[...] = x

    wg, wu, wd = wg_ref[...], wu_ref[...], wd_ref[...]
    if emit_bf16:
        wg, wu, wd = wg.astype(BF16), wu.astype(BF16), wd.astype(BF16)
        for ref, val in zip(wb_refs, (wg, wu, wd)):
            ref[...] = val
    h = h_scr[...]
    g = _dot(h, wg)
    u = _dot(h, wu)
    a = (0.5 * g * jax.nn.sigmoid(g) * u).astype(BF16)
    o_ref[...] += _dot(a, wd)

    if emit_next:
        @pl.when(j == pl.num_programs(1) - 1)
        def _():
            hn_ref[...] = _rms(o_ref[...], nn_ref[...]).astype(BF16)


def _ffn(x, norm, wg, wu, wd, next_norm, emit_next, layer=None, bm_pref=512, bf_pref=512):
    m, d = x.shape
    f = wg.shape[-1]
    emit_bf16 = layer is not None
    bm, bf = _blk(m, bm_pref), _blk(f, bf_pref)
    row = pl.BlockSpec((bm, d), lambda i, j: (i, 0))
    x_spec = pl.BlockSpec((bm, d), lambda i, j: (i, 0), pipeline_mode=pl.Buffered(1))
    w_specs = [pl.BlockSpec((d, bf), lambda i, j: (0, j)),
               pl.BlockSpec((d, bf), lambda i, j: (0, j)),
               pl.BlockSpec((bf, d), lambda i, j: (j, 0))]
    w_in_specs = w_specs
    out_shape = [jax.ShapeDtypeStruct((m, d), F32)]
    out_specs = [row]
    if emit_next:
        out_shape.append(jax.ShapeDtypeStruct((m, d), BF16))
        out_specs.append(row)
    if emit_bf16:
        assert m == bm, "each weight block must be visited exactly once"
        w_in_specs = [pl.BlockSpec((None, d, bf), lambda i, j: (layer, 0, j)),
                      pl.BlockSpec((None, d, bf), lambda i, j: (layer, 0, j)),
                      pl.BlockSpec((None, bf, d), lambda i, j: (layer, j, 0))]
        out_shape += [jax.ShapeDtypeStruct(w.shape[1:], BF16) for w in (wg, wu, wd)]
        out_specs += w_specs
    return pl.pallas_call(
        functools.partial(_ffn_kernel, emit_next=emit_next, emit_bf16=emit_bf16),
        grid=(m // bm, f // bf),
        in_specs=[x_spec, pl.BlockSpec((1, d), lambda i, j: (0, 0))] + w_in_specs
                 + [pl.BlockSpec((1, d), lambda i, j: (0, 0))],
        out_specs=out_specs,
        out_shape=out_shape,
        scratch_shapes=[pltpu.VMEM((bm, d), BF16)],
        name="swiglu_block",
        compiler_params=_params("parallel", "arbitrary"),
    )(x, norm, wg, wu, wd, next_norm)


def _mm_kernel(*refs, act, has_scale, has_res):
    h_ref, w_ref = refs[0], refs[1]
    k = 2
    acc = _dot(h_ref[...], w_ref[...])
    if has_scale:
        acc = acc * refs[k][...]
        k += 1
    if act == "sigmoid":
        acc = jax.nn.sigmoid(acc)
    if has_res:
        acc = refs[k][...] + acc
        k += 1
    o_ref = refs[k]
    o_ref[...] = acc.astype(o_ref.dtype)


def _mm(h, w, out_dtype, act=None, scale=None, res=None, bm_pref=1024, bn_pref=1024):
    m, kd = h.shape
    n = w.shape[1]
    bm, bn = _blk(m, bm_pref), _blk(n, bn_pref)
    args = [h, w]
    in_specs = [pl.BlockSpec((bm, kd), lambda j, i: (i, 0)),
                pl.BlockSpec((kd, bn), lambda j, i: (0, j))]
    if scale is not None:
        args.append(scale)
        in_specs.append(pl.BlockSpec((1, bn), lambda j, i: (0, j)))
    if res is not None:
        args.append(res)
        in_specs.append(pl.BlockSpec((bm, bn), lambda j, i: (i, j)))
    return pl.pallas_call(
        functools.partial(_mm_kernel, act=act, has_scale=scale is not None, has_res=res is not None),
        grid=(n // bn, m // bm),
        in_specs=in_specs,
        out_specs=pl.BlockSpec((bm, bn), lambda j, i: (i, j)),
        out_shape=jax.ShapeDtypeStruct((m, n), out_dtype),
        name="proj_" + (act or "lin") + ("_res" if res is not None else ""),
        compiler_params=_params("parallel", "parallel"),
    )(*args)


def _conv_prompt_kernel(h_ref, wu_ref, wb_ref, wc_ref, cw_ref, y_ref, tail_ref, carry_scr,
                        *, blocks_per_seq):
    i = pl.program_id(1)

    @pl.when(i % blocks_per_seq == 0)
    def _():
        carry_scr[...] = jnp.zeros_like(carry_scr)

    h = h_ref[...]
    u = _dot(h, wc_ref[...]) * _dot(h, wu_ref[...])
    cb = _dot(h, wb_ref[...])
    bm = u.shape[0]
    row = lax.broadcasted_iota(jnp.int32, u.shape, 0)
    c1 = carry_scr[7:8, :]
    c2 = carry_scr[6:7, :]
    u1 = jnp.where(row >= 1, pltpu.roll(u, 1, 0), c1)
    u2 = jnp.where(row >= 2, pltpu.roll(u, 2, 0), jnp.where(row == 0, c2, c1))
    cw = cw_ref[...]
    y = cb * (cw[0:1, :] * u2 + cw[1:2, :] * u1 + cw[2:3, :] * u)
    y_ref[...] = y.astype(BF16)
    carry_scr[...] = u[bm - 8:, :]
    tail_ref[0] = u[bm - 2:, :]


def _conv_prompt(h, w_conv, conv_w, n_seq, bm_pref=1024, bn_pref=512):
    m, kd = h.shape
    c = conv_w.shape[1]
    t = m // n_seq
    bm, bn = _blk(t, bm_pref), _blk(c, bn_pref)
    nb = c // bn
    bps = t // bm
    return pl.pallas_call(
        functools.partial(_conv_prompt_kernel, blocks_per_seq=bps),
        grid=(nb, m // bm),
        in_specs=[
            pl.BlockSpec((bm, kd), lambda j, i: (i, 0)),
            pl.BlockSpec((kd, bn), lambda j, i: (0, j)),
            pl.BlockSpec((kd, bn), lambda j, i: (0, j + nb)),
            pl.BlockSpec((kd, bn), lambda j, i: (0, j + 2 * nb)),
            pl.BlockSpec((conv_w.shape[0], bn), lambda j, i: (0, j)),
        ],
        out_specs=[
            pl.BlockSpec((bm, bn), lambda j, i: (i, j)),
            pl.BlockSpec((1, 2, bn), lambda j, i: (i // bps, 0, j)),
        ],
        out_shape=[jax.ShapeDtypeStruct((m, c), BF16),
                   jax.ShapeDtypeStruct((n_seq, 2, c), F32)],
        scratch_shapes=[pltpu.VMEM((8, bn), F32)],
        name="conv_prompt",
        compiler_params=_params("parallel", "arbitrary"),
    )(h, w_conv, w_conv, w_conv, conv_w)


def _conv_decode_kernel(h_ref, wu_ref, wb_ref, wc_ref, cw_ref, b0_ref, b1_ref, y_ref, u_ref):
    h = h_ref[...]
    u = _dot(h, wc_ref[...]) * _dot(h, wu_ref[...])
    cb = _dot(h, wb_ref[...])
    cw = cw_ref[...]
    y = cb * (cw[0:1, :] * b0_ref[...] + cw[1:2, :] * b1_ref[...] + cw[2:3, :] * u)
    y_ref[...] = y.astype(BF16)
    u_ref[...] = u


def _conv_decode(h, w_conv, conv_w, buf0, buf1, bn_pref=512):
    m, kd = h.shape
    c = conv_w.shape[1]
    bn = _blk(c, bn_pref)
    nb = c // bn
    return pl.pallas_call(
        _conv_decode_kernel,
        grid=(nb,),
        in_specs=[
            pl.BlockSpec((m, kd), lambda j: (0, 0)),
            pl.BlockSpec((kd, bn), lambda j: (0, j)),
            pl.BlockSpec((kd, bn), lambda j: (0, j + nb)),
            pl.BlockSpec((kd, bn), lambda j: (0, j + 2 * nb)),
            pl.BlockSpec((conv_w.shape[0], bn), lambda j: (0, j)),
            pl.BlockSpec((m, bn), lambda j: (0, j)),
            pl.BlockSpec((m, bn), lambda j: (0, j)),
        ],
        out_specs=[pl.BlockSpec((m, bn), lambda j: (0, j)),
                   pl.BlockSpec((m, bn), lambda j: (0, j))],
        out_shape=[jax.ShapeDtypeStruct((m, c), BF16), jax.ShapeDtypeStruct((m, c), F32)],
        name="conv_decode",
        compiler_params=_params("parallel"),
    )(h, w_conv, w_conv, w_conv, conv_w, buf0, buf1)


def _fox_in_kernel(h_ref, wq_ref, wk_ref, wv_ref, qn_ref, kn_ref, *rest, dh, q_scale):
    q_ref, kb_ref, vb_ref, kf_ref, vf_ref = rest[-5:]
    h = h_ref[...]
    fq = _dot(h, wq_ref[...])
    fk = _dot(h, wk_ref[...])
    fv = _dot(h, wv_ref[...])
    for hh in range(fq.shape[1] // dh):
        sl = slice(hh * dh, (hh + 1) * dh)
        qh = _rms(fq[:, sl], qn_ref[...])
        kh = _rms(fk[:, sl], kn_ref[...])
        q_ref[:, sl] = (qh * q_scale).astype(BF16)
        kf_ref[:, sl] = kh
        kb_ref[:, sl] = kh.astype(BF16)
    vf_ref[...] = fv
    vb_ref[...] = fv.astype(BF16)


def _fox_in(h, w_fox, qn, kn, dh, layer, depth, stacks=None, bm_pref=1024, bn_pref=512):
    m, kd = h.shape
    c = w_fox.shape[1] // 3
    bm, bn = _blk(m, bm_pref), _blk(c, bn_pref)
    nb = c // bn
    o_spec = pl.BlockSpec((bm, bn), lambda j, i: (i, j))
    s_spec = pl.BlockSpec((None, bm, bn), lambda j, i: (layer, i, j))
    args = [h, w_fox, w_fox, w_fox, qn, kn]
    in_specs = [
        pl.BlockSpec((bm, kd), lambda j, i: (i, 0)),
        pl.BlockSpec((kd, bn), lambda j, i: (0, j)),
        pl.BlockSpec((kd, bn), lambda j, i: (0, j + nb)),
        pl.BlockSpec((kd, bn), lambda j, i: (0, j + 2 * nb)),
        pl.BlockSpec((1, dh), lambda j, i: (0, 0)),
        pl.BlockSpec((1, dh), lambda j, i: (0, 0)),
    ]
    aliases = {}
    if stacks is not None:
        aliases = {len(args): 3, len(args) + 1: 4}
        args += list(stacks)
        in_specs += [pl.BlockSpec(memory_space=pl.ANY)] * 2
    return pl.pallas_call(
        functools.partial(_fox_in_kernel, dh=dh, q_scale=dh ** -0.5 * LOG2E),
        grid=(nb, m // bm),
        in_specs=in_specs,
        out_specs=[o_spec] * 3 + [s_spec] * 2,
        out_shape=[jax.ShapeDtypeStruct((m, c), BF16)] * 3
                  + [jax.ShapeDtypeStruct((depth, m, c), F32)] * 2,
        input_output_aliases=aliases,
        name="fox_qkv",
        compiler_params=_params("parallel", "parallel"),
    )(*args)


def _mlstm_prompt_kernel(qkv_ref, og_ref, z_ref, b_ref, mn_ref,
                         y_ref, c_ref, n_ref, m_ref, *, nh, dk, dv):
    @pl.when(pl.program_id(1) == 0)
    def _():
        c_ref[...] = jnp.zeros_like(c_ref)
        n_ref[...] = jnp.zeros_like(n_ref)
        m_ref[...] = jnp.zeros_like(m_ref)

    z = z_ref[...] + b_ref[...]
    L = z.shape[0]
    row = lax.broadcasted_iota(jnp.int32, z.shape, 0)
    lane = lax.broadcasted_iota(jnp.int32, z.shape, 1)
    bt = _log_sigmoid(z)
    s = 1
    while s < L:
        bt = bt + jnp.where(row >= s, pltpu.roll(bt, s, 0), 0.0)
        s *= 2
    g = jnp.where(lane < nh, z, bt)
    g_t = g.T
    causal = (lax.broadcasted_iota(jnp.int32, (L, L), 0)
              >= lax.broadcasted_iota(jnp.int32, (L, L), 1))
    for h in range(nh):
        q = qkv_ref[:, h * dk:(h + 1) * dk]
        k = qkv_ref[:, nh * dk + h * dk: nh * dk + (h + 1) * dk]
        v = qkv_ref[:, 2 * nh * dk + h * dv: 2 * nh * dk + (h + 1) * dv]
        ig_col, bt_col = g[:, h:h + 1], g[:, nh + h:nh + h + 1]
        ig_row, bt_row = g_t[h:h + 1, :], g_t[nh + h:nh + h + 1, :]
        m_prev = m_ref[0, h:h + 1, 0:1]
        dmat = jnp.where(causal, bt_col - bt_row + ig_row, -jnp.inf)
        inter = bt_col + m_prev
        m_t = jnp.maximum(inter, jnp.max(dmat, axis=1, keepdims=True))
        w = jnp.exp(dmat - m_t)
        a = jnp.exp(inter - m_t)
        sc = lax.dot_general(q, k, NT_DIMS, preferred_element_type=F32) * w
        c_old = c_ref[0, h]
        n_old = n_ref[0, h:h + 1, :]
        num = a * _dot(q, c_old.astype(BF16)) + _dot(sc.astype(BF16), v)
        den = (a * jnp.sum(q.astype(F32) * n_old, axis=1, keepdims=True)
               + jnp.sum(sc, axis=1, keepdims=True))
        hm = num / jnp.maximum(jnp.abs(den), jnp.exp(-m_t))
        m_last = m_t[L - 1:L, :]
        a_last = a[L - 1:L, :]
        w_last = jnp.exp(bt_col[L - 1:L, :] - bt_col + ig_col - m_last)
        kw = k.astype(F32) * w_last
        c_ref[0, h] = a_last * c_old + _dot(kw.T.astype(BF16), v)
        n_ref[0, h:h + 1, :] = a_last * n_old + jnp.sum(kw, axis=0, keepdims=True)
        m_ref[0, h:h + 1, :] = jnp.broadcast_to(m_last, (1, m_ref.shape[2]))
        vs = slice(h * dv, (h + 1) * dv)
        y_ref[:, vs] = (og_ref[:, vs].astype(F32) * _rms(hm, mn_ref[:, vs])).astype(BF16)


def _mlstm_prompt(qkv, og, z, bias, m_norm, n_seq, nh, dk, dv):
    m = qkv.shape[0]
    t = m // n_seq
    L = M_CHUNK if t % M_CHUNK == 0 else t
    nc = t // L
    bw = nh * dv
    return pl.pallas_call(
        functools.partial(_mlstm_prompt_kernel, nh=nh, dk=dk, dv=dv),
        grid=(n_seq, nc),
        in_specs=[
            pl.BlockSpec((L, qkv.shape[1]), lambda n, c: (n * nc + c, 0)),
            pl.BlockSpec((L, bw), lambda n, c: (n * nc + c, 0)),
            pl.BlockSpec((L, LANES), lambda n, c: (n * nc + c, 0)),
            pl.BlockSpec((1, LANES), lambda n, c: (0, 0)),
            pl.BlockSpec((1, bw), lambda n, c: (0, 0)),
        ],
        out_specs=[
            pl.BlockSpec((L, bw), lambda n, c: (n * nc + c, 0)),
            pl.BlockSpec((1, nh, dk, dv), lambda n, c: (n, 0, 0, 0)),
            pl.BlockSpec((1, nh, dk), lambda n, c: (n, 0, 0)),
            pl.BlockSpec((1, 8, LANES), lambda n, c: (n, 0, 0)),
        ],
        out_shape=[
            jax.ShapeDtypeStruct((m, bw), BF16),
            jax.ShapeDtypeStruct((n_seq, nh, dk, dv), F32),
            jax.ShapeDtypeStruct((n_seq, nh, dk), F32),
            jax.ShapeDtypeStruct((n_seq, 8, LANES), F32),
        ],
        name="mlstm_prompt",
        compiler_params=_params("parallel", "arbitrary"),
    )(qkv, og, z, bias, m_norm)


def _mlstm_decode_kernel(qkv_ref, og_ref, z_ref, b_ref, mn_ref, c_ref, n_ref, m_ref,
                         *rest, nh, dk, dv):
    y_ref, co_ref, no_ref, mo_ref = rest[-4:]
    bs = qkv_ref.shape[0]
    qkv = qkv_ref[...].astype(F32)
    og = og_ref[...].astype(F32)
    z = z_ref[...] + b_ref[...]
    lf_all = _log_sigmoid(z)
    eye = (lax.broadcasted_iota(jnp.int32, (dk, dk), 0)
           == lax.broadcasted_iota(jnp.int32, (dk, dk), 1))
    for s in range(bs):
        for h in range(nh):
            q = qkv[s:s + 1, h * dk:(h + 1) * dk]
            k = qkv[s:s + 1, nh * dk + h * dk: nh * dk + (h + 1) * dk]
            v = qkv[s:s + 1, 2 * nh * dk + h * dv: 2 * nh * dk + (h + 1) * dv]
            ig = z[s:s + 1, h:h + 1]
            lf = lf_all[s:s + 1, nh + h:nh + h + 1]
            m_old = m_ref[s:s + 1, h:h + 1]
            c_old = c_ref[s, h]
            n_old = n_ref[s, h:h + 1, :]
            inter = lf + m_old
            m_t = jnp.maximum(inter, ig)
            w = jnp.exp(ig - m_t)
            a = jnp.exp(inter - m_t)
            sc = jnp.sum(q * k, axis=1, keepdims=True) * w
            num = a * _dot(q.astype(BF16), c_old.astype(BF16)) + sc * v
            den = a * jnp.sum(q * n_old, axis=1, keepdims=True) + sc
            hm = num / jnp.maximum(jnp.abs(den), jnp.exp(-m_t))
            k_col = jnp.sum(jnp.where(eye, k, 0.0), axis=1, keepdims=True)
            co_ref[s, h] = a * c_old + (w * k_col) * v
            no_ref[s, h:h + 1, :] = a * n_old + w * k
            mo_ref[s:s + 1, h:h + 1] = m_t
            vs = slice(h * dv, (h + 1) * dv)
            y_ref[s:s + 1, vs] = (og[s:s + 1, vs] * _rms(hm, mn_ref[:, vs])).astype(BF16)


def _mlstm_decode(qkv, og, z, bias, m_norm, c_all, n, m_state, layer, c_stack=None):
    nd = qkv.shape[0]
    nh, dk, dv = c_all.shape[2:]
    bs = _blk(nd, 8)
    bw = nh * dv
    row = lambda i: (i, 0)
    c_spec = pl.BlockSpec((None, bs, nh, dk, dv), lambda i: (layer, i, 0, 0, 0))
    args = [qkv, og, z, bias, m_norm, c_all, n, m_state]
    in_specs = [
        pl.BlockSpec((bs, qkv.shape[1]), row),
        pl.BlockSpec((bs, bw), row),
        pl.BlockSpec((bs, LANES), row),
        pl.BlockSpec((1, LANES), lambda i: (0, 0)),
        pl.BlockSpec((1, bw), lambda i: (0, 0)),
        c_spec,
        pl.BlockSpec((bs, nh, dk), lambda i: (i, 0, 0)),
        pl.BlockSpec((bs, nh), row),
    ]
    aliases = {}
    if c_stack is not None:
        aliases = {len(args): 1}
        args.append(c_stack)
        in_specs.append(pl.BlockSpec(memory_space=pl.ANY))
    return pl.pallas_call(
        functools.partial(_mlstm_decode_kernel, nh=nh, dk=dk, dv=dv),
        grid=(nd // bs,),
        in_specs=in_specs,
        out_specs=[
            pl.BlockSpec((bs, bw), row),
            c_spec,
            pl.BlockSpec((bs, nh, dk), lambda i: (i, 0, 0)),
            pl.BlockSpec((bs, nh), row),
        ],
        out_shape=[
            jax.ShapeDtypeStruct((nd, bw), BF16),
            jax.ShapeDtypeStruct(c_all.shape, F32),
            jax.ShapeDtypeStruct(n.shape, F32),
            jax.ShapeDtypeStruct(m_state.shape, F32),
        ],
        input_output_aliases=aliases,
        name="mlstm_decode",
        compiler_params=_params("parallel"),
    )(*args)


def _split3(x):
    hi = x.astype(BF16).astype(F32)
    mid = (x - hi).astype(BF16).astype(F32)
    lo = (x - hi - mid).astype(BF16).astype(F32)
    return hi, mid, lo


def _fgate_kernel(z_ref, b_ref, lf_ref, qa_ref, ka_ref, *, nh, gate_col):
    lf = _log_sigmoid(z_ref[...] + b_ref[...])
    lf_ref[...] = lf
    t = lf.shape[0]
    row = lax.broadcasted_iota(jnp.int32, lf.shape, 0)
    lane = lax.broadcasted_iota(jnp.int32, lf.shape, 1)
    acc = lf
    s = 1
    while s < t:
        acc = acc + jnp.where(row >= s, pltpu.roll(acc, s, 0), 0.0)
        s *= 2
    acc = acc * LOG2E
    for h in range(nh):
        hi, mid, lo = _split3(acc[:, gate_col + h:gate_col + h + 1])
        terms = jnp.where(lane % 3 == 0, hi, jnp.where(lane % 3 == 1, mid, lo))
        sl = slice(h * LANES, (h + 1) * LANES)
        qa_ref[:, sl] = jnp.where(lane < 3, terms, jnp.where(lane < 6, 1.0, 0.0)).astype(BF16)
        ka_ref[:, sl] = jnp.where(lane < 3, 1.0, jnp.where(lane < 6, -terms, 0.0)).astype(BF16)


def _fgate(z, bias, n_seq, nh, gate_col):
    m = z.shape[0]
    t = m // n_seq
    spec = pl.BlockSpec((t, LANES), lambda n: (n, 0))
    aug = pl.BlockSpec((t, nh * LANES), lambda n: (n, 0))
    return pl.pallas_call(
        functools.partial(_fgate_kernel, nh=nh, gate_col=gate_col),
        grid=(n_seq,),
        in_specs=[spec, pl.BlockSpec((1, LANES), lambda n: (0, 0))],
        out_specs=[spec, aug, aug],
        out_shape=[jax.ShapeDtypeStruct((m, LANES), F32)]
                  + [jax.ShapeDtypeStruct((m, nh * LANES), BF16)] * 2,
        name="fox_gate_cumsum",
        compiler_params=_params("parallel"),
    )(z, bias)


def _attn_kernel(q_ref, qa_ref, k_ref, ka_ref, v_ref, o_ref, m_scr, l_scr, acc_scr, *, nh, dh):
    i, j = pl.program_id(1), pl.program_id(2)
    bq, bk = q_ref.shape[0], k_ref.shape[0]
    nrep = bk // LANES

    @pl.when(j == 0)
    def _():
        m_scr[...] = jnp.full_like(m_scr, -jnp.inf)
        l_scr[...] = jnp.zeros_like(l_scr)
        acc_scr[...] = jnp.zeros_like(acc_scr)

    def update(on_diagonal):
        if on_diagonal:
            mask = (lax.broadcasted_iota(jnp.int32, (bq, bk), 0)
                    >= lax.broadcasted_iota(jnp.int32, (bq, bk), 1))
        ones = jnp.ones((bk, LANES), BF16)
        for h in range(nh):
            sl = slice(h * dh, (h + 1) * dh)
            s = lax.dot_general(jnp.concatenate([q_ref[:, sl], qa_ref[:, sl]], axis=1),
                                jnp.concatenate([k_ref[:, sl], ka_ref[:, sl]], axis=1),
                                NT_DIMS, preferred_element_type=F32)
            if on_diagonal:
                s = jnp.where(mask, s, -jnp.inf)
            m_prev = m_scr[h]
            m_new = jnp.maximum(m_prev, jnp.max(s, axis=1, keepdims=True))
            p = jnp.exp2(s - jnp.concatenate([m_new] * nrep, axis=1))
            alpha = jnp.exp2(m_prev - m_new)
            pv = _dot(p.astype(BF16), jnp.concatenate([v_ref[:, sl], ones], axis=1))
            l_scr[h] = alpha * l_scr[h] + pv[:, dh:]
            acc_scr[:, sl] = alpha * acc_scr[:, sl] + pv[:, :dh]
            m_scr[h] = m_new

    @pl.when(j < i)
    def _():
        update(False)

    @pl.when(j == i)
    def _():
        update(True)

    @pl.when(j == pl.num_programs(2) - 1)
    def _():
        for h in range(nh):
            sl = slice(h * dh, (h + 1) * dh)
            o_ref[:, sl] = (acc_scr[:, sl] / l_scr[h]).astype(BF16)


def _attn(q, qa, k, ka, v, n_seq, nh, dh, blk_pref=512):
    m, c = q.shape
    t = m // n_seq
    assert dh == LANES, "running softmax stats and the bias columns are one head-dim wide"
    bq = bk = _blk(t, blk_pref)
    nq = nk = t // bq
    q_spec = pl.BlockSpec((bq, c), lambda b, i, j: (b * nq + i, 0))
    kv_spec = pl.BlockSpec((bk, c), lambda b, i, j: (b * nk + jnp.minimum(j, i), 0))
    return pl.pallas_call(
        functools.partial(_attn_kernel, nh=nh, dh=dh),
        grid=(n_seq, nq, nk),
        in_specs=[q_spec, q_spec, kv_spec, kv_spec, kv_spec],
        out_specs=q_spec,
        out_shape=jax.ShapeDtypeStruct((m, c), BF16),
        scratch_shapes=[pltpu.VMEM((nh, bq, LANES), F32), pltpu.VMEM((nh, bq, LANES), F32),
                        pltpu.VMEM((bq, c), F32)],
        name="fox_prompt_attn",
        compiler_params=_params("parallel", "parallel", "arbitrary"),
    )(q, qa, k, ka, v)


def _decode_attn_kernel(pt_ref, q_ref, kn_ref, vn_ref, z_ref, b_ref, *rest,
                        nh, dh, gate_col, pps):
    k_refs, v_refs, lf_refs = rest[:pps], rest[pps:2 * pps], rest[2 * pps:3 * pps]
    o_ref, lfn_ref, lf_scr, m_scr, l_scr, acc_scr, carry_scr = rest[3 * pps:]
    step = pl.program_id(1)
    page = k_refs[0].shape[0]
    w = page * nh
    q = q_ref[0]
    qb = q.astype(BF16)
    lf_new = _log_sigmoid(z_ref[0] + b_ref[...])
    lfn_ref[0] = lf_new
    pick = (lax.broadcasted_iota(jnp.int32, (nh, LANES), 1)
            == lax.broadcasted_iota(jnp.int32, (nh, LANES), 0) + gate_col)
    fn_col = jnp.sum(jnp.where(pick, jnp.broadcast_to(lf_new, (nh, LANES)), 0.0),
                     axis=1, keepdims=True) * LOG2E
    own_head = (lax.broadcasted_iota(jnp.int32, (nh, w), 1) % nh
                == lax.broadcasted_iota(jnp.int32, (nh, w), 0))

    @pl.when(step == 0)
    def _():
        m_scr[...] = jnp.full_like(m_scr, -jnp.inf)
        l_scr[...] = jnp.zeros_like(l_scr)
        acc_scr[...] = jnp.zeros_like(acc_scr)
        carry_scr[...] = jnp.zeros_like(carry_scr)

    for r in range(pps):
        lf_scr[r:r + 1, :] = lf_refs[r][...]
    lf = lf_scr[...]
    pos = lax.broadcasted_iota(jnp.int32, lf.shape, 1)
    rowi = lax.broadcasted_iota(jnp.int32, lf.shape, 0)
    suf, tot = lf, lf
    sh = nh
    while sh < w:
        suf = suf + jnp.where(pos < w - sh, pltpu.roll(suf, w - sh, 1), 0.0)
        tot = tot + pltpu.roll(tot, sh, 1)
        sh *= 2
    newer = tot
    sh = 1
    while sh < pps:
        newer = newer + jnp.where(rowi >= sh, pltpu.roll(newer, sh, 0), 0.0)
        sh *= 2
    bias = (suf - lf + newer - tot + carry_scr[...]) * LOG2E

    scores = []
    for r in range(pps):
        k2 = k_refs[r][...].reshape(w, dh).astype(BF16)
        s = lax.dot_general(qb, k2, NT_DIMS, preferred_element_type=F32)
        scores.append(jnp.where(own_head, s + fn_col + bias[r:r + 1, :], -jnp.inf))
    m_prev = m_scr[...]
    m_new = jnp.maximum(m_prev, jnp.max(functools.reduce(jnp.maximum, scores),
                                        axis=1, keepdims=True))
    alpha = jnp.exp2(m_prev - m_new)
    l_run = alpha * l_scr[...]
    acc = alpha * acc_scr[...]
    for r in range(pps):
        pe = jnp.exp2(scores[r] - m_new)
        l_run = l_run + jnp.sum(pe, axis=1, keepdims=True)
        acc = acc + _dot(pe.astype(BF16), v_refs[r][...].reshape(w, dh).astype(BF16))
    m_scr[...] = m_new
    l_scr[...] = l_run
    acc_scr[...] = acc
    carry_scr[...] = carry_scr[...] + newer[pps - 1:pps, :]

    @pl.when(step == pl.num_programs(1) - 1)
    def _():
        s_new = jnp.sum(q * kn_ref[0], axis=1, keepdims=True)
        m_fin = jnp.maximum(m_new, s_new)
        al = jnp.exp2(m_new - m_fin)
        pn = jnp.exp2(s_new - m_fin)
        o_ref[0] = ((al * acc + pn * vn_ref[0]) / (al * l_run + pn)).astype(o_ref.dtype)


def _decode_attn(page_table, q, k_new, v_new, z, bias, cache_k, cache_v, cache_lf,
                 layer, gate_col, pps_pref=16):
    nd, nh, dh = q.shape
    n_pages = page_table.shape[1]
    page = cache_k.shape[2]
    w = page * nh
    assert (page & (page - 1)) == 0, "per-head scans over a page use power-of-two strides"
    pps = _blk(n_pages, pps_pref)
    assert (pps & (pps - 1)) == 0
    row3 = lambda n, p, pt: (n, 0, 0)

    def page_map(r, tail):
        return lambda n, p, pt: (layer, pt[n, n_pages - 1 - (p * pps + r)]) + tail

    kv_specs = [pl.BlockSpec((None, None, page, nh, dh), page_map(r, (0, 0, 0))) for r in range(pps)]
    lf_specs = [pl.BlockSpec((None, None, 1, w), page_map(r, (0, 0))) for r in range(pps)]
    grid_spec = pltpu.PrefetchScalarGridSpec(
        num_scalar_prefetch=1,
        grid=(nd, n_pages // pps),
        in_specs=[
            pl.BlockSpec((1, nh, dh), row3),
            pl.BlockSpec((1, nh, dh), row3),
            pl.BlockSpec((1, nh, dh), row3),
            pl.BlockSpec((1, 1, LANES), row3),
            pl.BlockSpec((1, LANES), lambda n, p, pt: (0, 0)),
        ] + kv_specs + kv_specs + lf_specs,
        out_specs=[pl.BlockSpec((1, nh, dh), row3), pl.BlockSpec((1, 1, LANES), row3)],
        scratch_shapes=[pltpu.VMEM((pps, w), F32), pltpu.VMEM((nh, 1), F32), pltpu.VMEM((nh, 1), F32),
                        pltpu.VMEM((nh, dh), F32), pltpu.VMEM((1, w), F32)],
    )
    out, lf_new = pl.pallas_call(
        functools.partial(_decode_attn_kernel, nh=nh, dh=dh, gate_col=gate_col, pps=pps),
        grid_spec=grid_spec,
        out_shape=[jax.ShapeDtypeStruct((nd, nh, dh), F32),
                   jax.ShapeDtypeStruct((nd, 1, LANES), F32)],
        name="fox_decode_attn",
        compiler_params=_params("parallel", "arbitrary"),
    )(page_table, q, k_new, v_new, z.reshape(nd, 1, LANES), bias,
      *([cache_k] * pps), *([cache_v] * pps), *([cache_lf] * pps))
    return out, lf_new.reshape(nd, LANES)


def _merge_kernel(h_ref, yc_ref, ym_ref, yf_ref, wg_ref, wb_ref, o_ref, acc_scr):
    b = pl.program_id(2)
    gate = jax.nn.sigmoid(_dot(h_ref[...], wg_ref[...]))

    @pl.when(b == 0)
    def _():
        acc_scr[...] = gate * _dot(yc_ref[...], wb_ref[...])

    @pl.when(b == 1)
    def _():
        acc_scr[...] += gate * _dot(ym_ref[...], wb_ref[...])

    @pl.when(b == 2)
    def _():
        o_ref[...] = (acc_scr[...] + gate * _dot(yf_ref[...], wb_ref[...])).astype(BF16)


def _merge(h, yc, ym, yf, w_gate, w_branch, bm_pref=1024, bn_pref=1024):
    m, d = h.shape
    bw = yc.shape[1]
    bm, bn = _blk(m, bm_pref), _blk(d, bn_pref)
    nc = d // bn
    y_spec = pl.BlockSpec((bm, bw), lambda i, c, b: (i, 0))
    return pl.pallas_call(
        _merge_kernel,
        grid=(m // bm, nc, 3),
        in_specs=[
            pl.BlockSpec((bm, d), lambda i, c, b: (i, 0)),
            y_spec, y_spec, y_spec,
            pl.BlockSpec((d, bn), lambda i, c, b: (0, b * nc + c)),
            pl.BlockSpec((None, bw, bn), lambda i, c, b: (b, 0, c)),
        ],
        out_specs=pl.BlockSpec((bm, bn), lambda i, c, b: (i, c)),
        out_shape=jax.ShapeDtypeStruct((m, d), BF16),
        scratch_shapes=[pltpu.VMEM((bm, bn), F32)],
        name="gated_merge",
        compiler_params=_params("parallel", "parallel", "arbitrary"),
    )(h, yc, ym, yf, w_gate, w_branch)


def kernel(x_prompt, x_sample, cache_k, cache_v, cache_logf, page_table, state_conv, state_C, state_n, state_m, norm_ffn1, ffn1_gate, ffn1_up, ffn1_down, norm_mix, w_in, conv_w, mlstm_b_i, mlstm_b_f, mlstm_norm, fox_b_f, fox_q_norm, fox_k_norm, w_branch, w_out, norm_ffn2, ffn2_gate, ffn2_up, ffn2_down):
    depth = w_in.shape[0]
    n_pr, t_pr, d = x_prompt.shape
    n_dec = x_sample.shape[0]
    conv_c = conv_w.shape[2]
    nh_m, dk, dv = state_C.shape[2:]
    n_pool, page, nh_f, dh = cache_k.shape[1:]
    fox_c = nh_f * dh
    assert x_sample.shape[1] == 1, "sample group is one new token per sequence"
    assert 2 * nh_m + nh_f <= LANES

    o_conv = 0
    o_mq = 3 * conv_c
    o_mo = o_mq + 2 * nh_m * dk + nh_m * dv
    o_mi = o_mo + nh_m * dv
    o_fq = o_mi + 2 * nh_m
    o_ff = o_fq + 3 * fox_c
    o_gz = o_ff + nh_f
    gate_col = 2 * nh_m

    xp = x_prompt.reshape(n_pr * t_pr, d)
    xs = x_sample.reshape(n_dec, d)
    clf = cache_logf.reshape(depth, n_pool, 1, page * nh_f)

    mqkv_scale = jnp.concatenate([jnp.ones((nh_m * dk,), F32),
                                  jnp.full((nh_m * dk,), dk ** -0.5, F32),
                                  jnp.ones((nh_m * dv,), F32)])[None, :]
    outs = {k: [] for k in ("lp", "ks", "vs", "ls", "cbp", "cbs", "cp", "np", "mp", "ns", "ms")}
    kv_stacks = None
    c_stack = None
    for l in range(depth):
        bf = lambda a: a.astype(BF16)
        wi = w_in[l]
        w_conv = bf(wi[:, o_conv:o_mq])
        w_mqkv = bf(wi[:, o_mq:o_mo])
        w_mo = bf(wi[:, o_mo:o_mi])
        w_small = bf(jnp.concatenate(
            [wi[:, o_mi:o_fq], wi[:, o_ff:o_gz],
             jnp.zeros((d, LANES - 2 * nh_m - nh_f), F32)], axis=1))
        w_fox = bf(wi[:, o_fq:o_ff])
        w_gate = bf(wi[:, o_gz:])
        w_br, w_o = bf(w_branch[l]), bf(w_out[l])
        n1, nm, n2 = norm_ffn1[l][None, :], norm_mix[l][None, :], norm_ffn2[l][None, :]
        small_bias = jnp.concatenate(
            [mlstm_b_i[l], mlstm_b_f[l], fox_b_f[l],
             jnp.zeros((LANES - 2 * nh_m - nh_f,), F32)])[None, :]
        m_norm = mlstm_norm[l][None, :]
        qn, kn = fox_q_norm[l][None, :], fox_k_norm[l][None, :]
        cw = conv_w[l]

        xs, hs, g1, u1, d1 = _ffn(xs, n1, ffn1_gate, ffn1_up, ffn1_down, nm, True, layer=l)
        yc_s, u_s = _conv_decode(hs, w_conv, cw, state_conv[l, :, 0, :], state_conv[l, :, 1, :])
        qkv_s = _mm(hs, w_mqkv, BF16, scale=mqkv_scale)
        og_s = _mm(hs, w_mo, BF16, act="sigmoid")
        z_s = _mm(hs, w_small, F32)
        q_s, _, _, kf_s, vf_s = _fox_in(hs, w_fox, qn, kn, dh, 0, 1)
        ym_s, c_stack, n_s, m_s = _mlstm_decode(qkv_s, og_s, z_s, small_bias, m_norm,
                                                state_C, state_n[l], state_m[l], l, c_stack)
        heads = lambda a: a.astype(F32).reshape(n_dec, nh_f, dh)
        yf_s, lf_s = _decode_attn(page_table, heads(q_s), heads(kf_s), heads(vf_s), z_s, small_bias,
                                  cache_k, cache_v, clf, l, gate_col)
        yf_s = yf_s.reshape(n_dec, fox_c).astype(BF16)
        mg_s = _merge(hs, yc_s, ym_s, yf_s, w_gate, w_br)
        xs = _mm(mg_s, w_o, F32, res=xs)
        xs, g2, u2, d2 = _ffn(xs, n2, ffn2_gate, ffn2_up, ffn2_down, n2, False, layer=l)

        xp, hp = _ffn(xp, n1, g1, u1, d1, nm, True, bm_pref=1024)
        yc_p, tail_p = _conv_prompt(hp, w_conv, cw, n_pr)
        qkv_p = _mm(hp, w_mqkv, BF16, scale=mqkv_scale)
        og_p = _mm(hp, w_mo, BF16, act="sigmoid")
        z_p = _mm(hp, w_small, F32)
        q_p, kb_p, vb_p, *kv_stacks = _fox_in(hp, w_fox, qn, kn, dh, l, depth, kv_stacks)
        ym_p, c_p, n_p, m_p = _mlstm_prompt(qkv_p, og_p, z_p, small_bias, m_norm, n_pr, nh_m, dk, dv)
        lf_p, qa_p, ka_p = _fgate(z_p, small_bias, n_pr, nh_f, gate_col)
        yf_p = _attn(q_p, qa_p, kb_p, ka_p, vb_p, n_pr, nh_f, dh)
        mg_p = _merge(hp, yc_p, ym_p, yf_p, w_gate, w_br)
        xp = _mm(mg_p, w_o, F32, res=xp)
        (xp,) = _ffn(xp, n2, g2, u2, d2, n2, False, bm_pref=1024)

        outs["lp"].append(lf_p[:, gate_col:gate_col + nh_f].reshape(n_pr, t_pr, nh_f))
        outs["ks"].append(kf_s.reshape(n_dec, 1, nh_f, dh))
        outs["vs"].append(vf_s.reshape(n_dec, 1, nh_f, dh))
        outs["ls"].append(lf_s[:, gate_col:gate_col + nh_f].reshape(n_dec, 1, nh_f))
        outs["cbp"].append(tail_p)
        outs["cbs"].append(jnp.stack([state_conv[l, :, 1, :], u_s], axis=1))
        outs["cp"].append(c_p)
        outs["np"].append(n_p)
        outs["mp"].append(m_p[:, :nh_m, 0])
        outs["ns"].append(n_s)
        outs["ms"].append(m_s)

    st = {k: jnp.stack(v) for k, v in outs.items()}
    k_prompt, v_prompt = (a.reshape(depth, n_pr, t_pr, nh_f, dh) for a in kv_stacks)
    return (xp.reshape(n_pr, t_pr, d), xs.reshape(n_dec, 1, d),
            k_prompt, v_prompt, st["lp"], st["ks"], st["vs"], st["ls"],
            st["cbp"], st["cbs"], st["cp"], st["np"], st["mp"], c_stack, st["ns"], st["ms"])
```

```python
import functools

import jax
import jax.numpy as jnp
from jax import lax
from jax.experimental import pallas as pl
from jax.experimental.pallas import tpu as pltpu

F32 = jnp.float32
BF16 = jnp.bfloat16
EPS = 1e-6
LANES = 128
VMEM_LIMIT = 56 * 1024 * 1024
M_CHUNK = 512
NT_DIMS = (((1,), (1,)), ((), ()))
LOG2E = 1.4426950408889634


def _params(*sem):
    return pltpu.CompilerParams(dimension_semantics=sem, vmem_limit_bytes=VMEM_LIMIT)


def _blk(n, pref):
    if n <= pref:
        return n
    b = pref
    while n % b:
        b //= 2
    return b


def _rms(x, w):
    return x * lax.rsqrt(jnp.mean(x * x, axis=-1, keepdims=True) + EPS) * w


def _dot(a, b):
    return jnp.dot(a, b, preferred_element_type=F32)


def _log_sigmoid(x):
    return jnp.minimum(x, 0.0) - jnp.log(1.0 + jnp.exp(-jnp.abs(x)))


FFN_BF = 512


def _ffn_kernel(x_ref, n_ref, *refs, emit_next, from_f32):
    refs = list(refs)
    w_refs = [refs.pop(0) for _ in range(3 if from_f32 else 2)]
    nn_ref, o_ref = refs.pop(0), refs.pop(0)
    hn_ref = refs.pop(0) if emit_next else None
    w_out_refs = [refs.pop(0), refs.pop(0)] if from_f32 else None
    (h_scr,) = refs
    j = pl.program_id(1)

    @pl.when(j == 0)
    def _():
        x = x_ref[...]
        h_scr[...] = _rms(x, n_ref[...]).astype(BF16)
        o_ref[...] = x

    if from_f32:
        wgu = jnp.concatenate([w_refs[0][...].astype(BF16), w_refs[1][...].astype(BF16)], axis=1)
        wd = w_refs[2][...].astype(BF16)
        w_out_refs[0][...] = wgu
        w_out_refs[1][...] = wd
    else:
        wgu, wd = w_refs[0][...], w_refs[1][...]
    bf = wd.shape[0]
    gu = _dot(h_scr[...], wgu)
    g, u = gu[:, :bf], gu[:, bf:]
    a = (0.5 * g * jax.nn.sigmoid(g) * u).astype(BF16)
    o_ref[...] += _dot(a, wd)

    if emit_next:
        @pl.when(j == pl.num_programs(1) - 1)
        def _():
            hn_ref[...] = _rms(o_ref[...], nn_ref[...]).astype(BF16)


def _ffn(x, norm, weights, next_norm, emit_next, layer=None, bm_pref=512):
    m, d = x.shape
    from_f32 = layer is not None
    f = weights[-1].shape[-2]
    bm, bf = _blk(m, bm_pref), _blk(f, FFN_BF)
    row = pl.BlockSpec((bm, d), lambda i, j: (i, 0))
    vec = pl.BlockSpec((1, d), lambda i, j: (0, 0))
    fused_specs = [pl.BlockSpec((d, 2 * bf), lambda i, j: (0, j)),
                   pl.BlockSpec((bf, d), lambda i, j: (j, 0))]
    out_shape = [jax.ShapeDtypeStruct((m, d), F32)]
    out_specs = [row]
    if emit_next:
        out_shape.append(jax.ShapeDtypeStruct((m, d), BF16))
        out_specs.append(row)
    if from_f32:
        assert m == bm, "each weight block must be visited exactly once"
        w_specs = [pl.BlockSpec((None, d, bf), lambda i, j: (layer, 0, j)),
                   pl.BlockSpec((None, d, bf), lambda i, j: (layer, 0, j)),
                   pl.BlockSpec((None, bf, d), lambda i, j: (layer, j, 0))]
        out_shape += [jax.ShapeDtypeStruct((d, 2 * f), BF16), jax.ShapeDtypeStruct((f, d), BF16)]
        out_specs += fused_specs
    else:
        w_specs = fused_specs
    return pl.pallas_call(
        functools.partial(_ffn_kernel, emit_next=emit_next, from_f32=from_f32),
        grid=(m // bm, f // bf),
        in_specs=[row, vec] + w_specs + [vec],
        out_specs=out_specs,
        out_shape=out_shape,
        scratch_shapes=[pltpu.VMEM((bm, d), BF16)],
        name="swiglu_block",
        compiler_params=_params("parallel", "arbitrary"),
    )(x, norm, *weights, next_norm)


def _mm_kernel(*refs, act, has_scale, has_res):
    h_ref, w_ref = refs[0], refs[1]
    k = 2
    acc = _dot(h_ref[...], w_ref[...])
    if has_scale:
        acc = acc * refs[k][...]
        k += 1
    if act == "sigmoid":
        acc = jax.nn.sigmoid(acc)
    if has_res:
        acc = refs[k][...] + acc
        k += 1
    o_ref = refs[k]
    o_ref[...] = acc.astype(o_ref.dtype)


def _mm(h, w, out_dtype, act=None, scale=None, res=None, bm_pref=1024, bn_pref=1024):
    m, kd = h.shape
    n = w.shape[1]
    bm, bn = _blk(m, bm_pref), _blk(n, bn_pref)
    args = [h, w]
    in_specs = [pl.BlockSpec((bm, kd), lambda j, i: (i, 0)),
                pl.BlockSpec((kd, bn), lambda j, i: (0, j))]
    if scale is not None:
        args.append(scale)
        in_specs.append(pl.BlockSpec((1, bn), lambda j, i: (0, j)))
    if res is not None:
        args.append(res)
        in_specs.append(pl.BlockSpec((bm, bn), lambda j, i: (i, j)))
    return pl.pallas_call(
        functools.partial(_mm_kernel, act=act, has_scale=scale is not None, has_res=res is not None),
        grid=(n // bn, m // bm),
        in_specs=in_specs,
        out_specs=pl.BlockSpec((bm, bn), lambda j, i: (i, j)),
        out_shape=jax.ShapeDtypeStruct((m, n), out_dtype),
        name="proj_" + (act or "lin") + ("_res" if res is not None else ""),
        compiler_params=_params("parallel", "parallel"),
    )(*args)


def _conv_prompt_kernel(h_ref, wu_ref, wb_ref, wc_ref, cw_ref, y_ref, tail_ref, carry_scr,
                        *, blocks_per_seq):
    i = pl.program_id(1)

    @pl.when(i % blocks_per_seq == 0)
    def _():
        carry_scr[...] = jnp.zeros_like(carry_scr)

    h = h_ref[...]
    u = _dot(h, wc_ref[...]) * _dot(h, wu_ref[...])
    cb = _dot(h, wb_ref[...])
    bm = u.shape[0]
    row = lax.broadcasted_iota(jnp.int32, u.shape, 0)
    c1 = carry_scr[7:8, :]
    c2 = carry_scr[6:7, :]
    u1 = jnp.where(row >= 1, pltpu.roll(u, 1, 0), c1)
    u2 = jnp.where(row >= 2, pltpu.roll(u, 2, 0), jnp.where(row == 0, c2, c1))
    cw = cw_ref[...]
    y = cb * (cw[0:1, :] * u2 + cw[1:2, :] * u1 + cw[2:3, :] * u)
    y_ref[...] = y.astype(BF16)
    carry_scr[...] = u[bm - 8:, :]
    tail_ref[0] = u[bm - 2:, :]


def _conv_prompt(h, w_conv, conv_w, n_seq, bm_pref=1024, bn_pref=512):
    m, kd = h.shape
    c = conv_w.shape[1]
    t = m // n_seq
    bm, bn = _blk(t, bm_pref), _blk(c, bn_pref)
    nb = c // bn
    bps = t // bm
    return pl.pallas_call(
        functools.partial(_conv_prompt_kernel, blocks_per_seq=bps),
        grid=(nb, m // bm),
        in_specs=[
            pl.BlockSpec((bm, kd), lambda j, i: (i, 0)),
            pl.BlockSpec((kd, bn), lambda j, i: (0, j)),
            pl.BlockSpec((kd, bn), lambda j, i: (0, j + nb)),
            pl.BlockSpec((kd, bn), lambda j, i: (0, j + 2 * nb)),
            pl.BlockSpec((conv_w.shape[0], bn), lambda j, i: (0, j)),
        ],
        out_specs=[
            pl.BlockSpec((bm, bn), lambda j, i: (i, j)),
            pl.BlockSpec((1, 2, bn), lambda j, i: (i // bps, 0, j)),
        ],
        out_shape=[jax.ShapeDtypeStruct((m, c), BF16),
                   jax.ShapeDtypeStruct((n_seq, 2, c), F32)],
        scratch_shapes=[pltpu.VMEM((8, bn), F32)],
        name="conv_prompt",
        compiler_params=_params("parallel", "arbitrary"),
    )(h, w_conv, w_conv, w_conv, conv_w)


def _conv_decode_kernel(h_ref, wu_ref, wb_ref, wc_ref, cw_ref, b0_ref, b1_ref, y_ref, u_ref):
    h = h_ref[...]
    u = _dot(h, wc_ref[...]) * _dot(h, wu_ref[...])
    cb = _dot(h, wb_ref[...])
    cw = cw_ref[...]
    y = cb * (cw[0:1, :] * b0_ref[...] + cw[1:2, :] * b1_ref[...] + cw[2:3, :] * u)
    y_ref[...] = y.astype(BF16)
    u_ref[...] = u


def _conv_decode(h, w_conv, conv_w, buf0, buf1, bn_pref=512):
    m, kd = h.shape
    c = conv_w.shape[1]
    bn = _blk(c, bn_pref)
    nb = c // bn
    return pl.pallas_call(
        _conv_decode_kernel,
        grid=(nb,),
        in_specs=[
            pl.BlockSpec((m, kd), lambda j: (0, 0)),
            pl.BlockSpec((kd, bn), lambda j: (0, j)),
            pl.BlockSpec((kd, bn), lambda j: (0, j + nb)),
            pl.BlockSpec((kd, bn), lambda j: (0, j + 2 * nb)),
            pl.BlockSpec((conv_w.shape[0], bn), lambda j: (0, j)),
            pl.BlockSpec((m, bn), lambda j: (0, j)),
            pl.BlockSpec((m, bn), lambda j: (0, j)),
        ],
        out_specs=[pl.BlockSpec((m, bn), lambda j: (0, j)),
                   pl.BlockSpec((m, bn), lambda j: (0, j))],
        out_shape=[jax.ShapeDtypeStruct((m, c), BF16), jax.ShapeDtypeStruct((m, c), F32)],
        name="conv_decode",
        compiler_params=_params("parallel"),
    )(h, w_conv, w_conv, w_conv, conv_w, buf0, buf1)


def _fox_in_kernel(h_ref, wq_ref, wk_ref, wv_ref, qn_ref, kn_ref, *rest, dh, q_scale):
    q_ref, kb_ref, vb_ref, kf_ref, vf_ref = rest[-5:]
    h = h_ref[...]
    fq = _dot(h, wq_ref[...])
    fk = _dot(h, wk_ref[...])
    fv = _dot(h, wv_ref[...])
    for hh in range(fq.shape[1] // dh):
        sl = slice(hh * dh, (hh + 1) * dh)
        qh = _rms(fq[:, sl], qn_ref[...])
        kh = _rms(fk[:, sl], kn_ref[...])
        q_ref[:, sl] = (qh * q_scale).astype(BF16)
        kf_ref[:, sl] = kh
        kb_ref[:, sl] = kh.astype(BF16)
    vf_ref[...] = fv
    vb_ref[...] = fv.astype(BF16)


def _fox_in(h, w_fox, qn, kn, dh, layer, depth, stacks=None, bm_pref=1024, bn_pref=512):
    m, kd = h.shape
    c = w_fox.shape[1] // 3
    bm, bn = _blk(m, bm_pref), _blk(c, bn_pref)
    nb = c // bn
    o_spec = pl.BlockSpec((bm, bn), lambda j, i: (i, j))
    s_spec = pl.BlockSpec((None, bm, bn), lambda j, i: (layer, i, j))
    args = [h, w_fox, w_fox, w_fox, qn, kn]
    in_specs = [
        pl.BlockSpec((bm, kd), lambda j, i: (i, 0)),
        pl.BlockSpec((kd, bn), lambda j, i: (0, j)),
        pl.BlockSpec((kd, bn), lambda j, i: (0, j + nb)),
        pl.BlockSpec((kd, bn), lambda j, i: (0, j + 2 * nb)),
        pl.BlockSpec((1, dh), lambda j, i: (0, 0)),
        pl.BlockSpec((1, dh), lambda j, i: (0, 0)),
    ]
    aliases = {}
    if stacks is not None:
        aliases = {len(args): 3, len(args) + 1: 4}
        args += list(stacks)
        in_specs += [pl.BlockSpec(memory_space=pl.ANY)] * 2
    return pl.pallas_call(
        functools.partial(_fox_in_kernel, dh=dh, q_scale=dh ** -0.5 * LOG2E),
        grid=(nb, m // bm),
        in_specs=in_specs,
        out_specs=[o_spec] * 3 + [s_spec] * 2,
        out_shape=[jax.ShapeDtypeStruct((m, c), BF16)] * 3
                  + [jax.ShapeDtypeStruct((depth, m, c), F32)] * 2,
        input_output_aliases=aliases,
        name="fox_qkv",
        compiler_params=_params("parallel", "parallel"),
    )(*args)


def _mlstm_prompt_kernel(qkv_ref, og_ref, z_ref, b_ref, mn_ref,
                         y_ref, c_ref, n_ref, m_ref, *, nh, dk, dv):
    @pl.when(pl.program_id(1) == 0)
    def _():
        c_ref[...] = jnp.zeros_like(c_ref)
        n_ref[...] = jnp.zeros_like(n_ref)
        m_ref[...] = jnp.zeros_like(m_ref)

    z = z_ref[...] + b_ref[...]
    L = z.shape[0]
    row = lax.broadcasted_iota(jnp.int32, z.shape, 0)
    lane = lax.broadcasted_iota(jnp.int32, z.shape, 1)
    bt = _log_sigmoid(z)
    s = 1
    while s < L:
        bt = bt + jnp.where(row >= s, pltpu.roll(bt, s, 0), 0.0)
        s *= 2
    g = jnp.where(lane < nh, z, bt)
    g_t = g.T
    causal = (lax.broadcasted_iota(jnp.int32, (L, L), 0)
              >= lax.broadcasted_iota(jnp.int32, (L, L), 1))
    for h in range(nh):
        q = qkv_ref[:, h * dk:(h + 1) * dk]
        k = qkv_ref[:, nh * dk + h * dk: nh * dk + (h + 1) * dk]
        v = qkv_ref[:, 2 * nh * dk + h * dv: 2 * nh * dk + (h + 1) * dv]
        ig_col, bt_col = g[:, h:h + 1], g[:, nh + h:nh + h + 1]
        ig_row, bt_row = g_t[h:h + 1, :], g_t[nh + h:nh + h + 1, :]
        m_prev = m_ref[0, h:h + 1, 0:1]
        dmat = jnp.where(causal, bt_col - bt_row + ig_row, -jnp.inf)
        inter = bt_col + m_prev
        m_t = jnp.maximum(inter, jnp.max(dmat, axis=1, keepdims=True))
        w = jnp.exp(dmat - m_t)
        a = jnp.exp(inter - m_t)
        sc = lax.dot_general(q, k, NT_DIMS, preferred_element_type=F32) * w
        c_old = c_ref[0, h]
        n_old = n_ref[0, h:h + 1, :]
        num = a * _dot(q, c_old.astype(BF16)) + _dot(sc.astype(BF16), v)
        den = (a * jnp.sum(q.astype(F32) * n_old, axis=1, keepdims=True)
               + jnp.sum(sc, axis=1, keepdims=True))
        hm = num / jnp.maximum(jnp.abs(den), jnp.exp(-m_t))
        m_last = m_t[L - 1:L, :]
        a_last = a[L - 1:L, :]
        w_last = jnp.exp(bt_col[L - 1:L, :] - bt_col + ig_col - m_last)
        kw = k.astype(F32) * w_last
        c_ref[0, h] = a_last * c_old + _dot(kw.T.astype(BF16), v)
        n_ref[0, h:h + 1, :] = a_last * n_old + jnp.sum(kw, axis=0, keepdims=True)
        m_ref[0, h:h + 1, :] = jnp.broadcast_to(m_last, (1, m_ref.shape[2]))
        vs = slice(h * dv, (h + 1) * dv)
        y_ref[:, vs] = (og_ref[:, vs].astype(F32) * _rms(hm, mn_ref[:, vs])).astype(BF16)


def _mlstm_prompt(qkv, og, z, bias, m_norm, n_seq, nh, dk, dv):
    m = qkv.shape[0]
    t = m // n_seq
    L = M_CHUNK if t % M_CHUNK == 0 else t
    nc = t // L
    bw = nh * dv
    return pl.pallas_call(
        functools.partial(_mlstm_prompt_kernel, nh=nh, dk=dk, dv=dv),
        grid=(n_seq, nc),
        in_specs=[
            pl.BlockSpec((L, qkv.shape[1]), lambda n, c: (n * nc + c, 0)),
            pl.BlockSpec((L, bw), lambda n, c: (n * nc + c, 0)),
            pl.BlockSpec((L, LANES), lambda n, c: (n * nc + c, 0)),
            pl.BlockSpec((1, LANES), lambda n, c: (0, 0)),
            pl.BlockSpec((1, bw), lambda n, c: (0, 0)),
        ],
        out_specs=[
            pl.BlockSpec((L, bw), lambda n, c: (n * nc + c, 0)),
            pl.BlockSpec((1, nh, dk, dv), lambda n, c: (n, 0, 0, 0)),
            pl.BlockSpec((1, nh, dk), lambda n, c: (n, 0, 0)),
            pl.BlockSpec((1, 8, LANES), lambda n, c: (n, 0, 0)),
        ],
        out_shape=[
            jax.ShapeDtypeStruct((m, bw), BF16),
            jax.ShapeDtypeStruct((n_seq, nh, dk, dv), F32),
            jax.ShapeDtypeStruct((n_seq, nh, dk), F32),
            jax.ShapeDtypeStruct((n_seq, 8, LANES), F32),
        ],
        name="mlstm_prompt",
        compiler_params=_params("parallel", "arbitrary"),
    )(qkv, og, z, bias, m_norm)


def _mlstm_decode_kernel(qkv_ref, og_ref, z_ref, b_ref, mn_ref, c_ref, n_ref, m_ref,
                         *rest, nh, dk, dv):
    y_ref, co_ref, no_ref, mo_ref = rest[-4:]
    bs = qkv_ref.shape[0]
    qkv = qkv_ref[...].astype(F32)
    og = og_ref[...].astype(F32)
    z = z_ref[...] + b_ref[...]
    lf_all = _log_sigmoid(z)
    eye = (lax.broadcasted_iota(jnp.int32, (dk, dk), 0)
           == lax.broadcasted_iota(jnp.int32, (dk, dk), 1))
    for s in range(bs):
        for h in range(nh):
            q = qkv[s:s + 1, h * dk:(h + 1) * dk]
            k = qkv[s:s + 1, nh * dk + h * dk: nh * dk + (h + 1) * dk]
            v = qkv[s:s + 1, 2 * nh * dk + h * dv: 2 * nh * dk + (h + 1) * dv]
            ig = z[s:s + 1, h:h + 1]
            lf = lf_all[s:s + 1, nh + h:nh + h + 1]
            m_old = m_ref[s:s + 1, h:h + 1]
            c_old = c_ref[s, h]
            n_old = n_ref[s, h:h + 1, :]
            inter = lf + m_old
            m_t = jnp.maximum(inter, ig)
            w = jnp.exp(ig - m_t)
            a = jnp.exp(inter - m_t)
            sc = jnp.sum(q * k, axis=1, keepdims=True) * w
            num = a * _dot(q.astype(BF16), c_old.astype(BF16)) + sc * v
            den = a * jnp.sum(q * n_old, axis=1, keepdims=True) + sc
            hm = num / jnp.maximum(jnp.abs(den), jnp.exp(-m_t))
            k_col = jnp.sum(jnp.where(eye, k, 0.0), axis=1, keepdims=True)
            co_ref[s, h] = a * c_old + (w * k_col) * v
            no_ref[s, h:h + 1, :] = a * n_old + w * k
            mo_ref[s:s + 1, h:h + 1] = m_t
            vs = slice(h * dv, (h + 1) * dv)
            y_ref[s:s + 1, vs] = (og[s:s + 1, vs] * _rms(hm, mn_ref[:, vs])).astype(BF16)


def _mlstm_decode(qkv, og, z, bias, m_norm, c_all, n, m_state, layer, c_stack=None):
    nd = qkv.shape[0]
    nh, dk, dv = c_all.shape[2:]
    bs = _blk(nd, 8)
    bw = nh * dv
    row = lambda i: (i, 0)
    c_spec = pl.BlockSpec((None, bs, nh, dk, dv), lambda i: (layer, i, 0, 0, 0))
    args = [qkv, og, z, bias, m_norm, c_all, n, m_state]
    in_specs = [
        pl.BlockSpec((bs, qkv.shape[1]), row),
        pl.BlockSpec((bs, bw), row),
        pl.BlockSpec((bs, LANES), row),
        pl.BlockSpec((1, LANES), lambda i: (0, 0)),
        pl.BlockSpec((1, bw), lambda i: (0, 0)),
        c_spec,
        pl.BlockSpec((bs, nh, dk), lambda i: (i, 0, 0)),
        pl.BlockSpec((bs, nh), row),
    ]
    aliases = {}
    if c_stack is not None:
        aliases = {len(args): 1}
        args.append(c_stack)
        in_specs.append(pl.BlockSpec(memory_space=pl.ANY))
    return pl.pallas_call(
        functools.partial(_mlstm_decode_kernel, nh=nh, dk=dk, dv=dv),
        grid=(nd // bs,),
        in_specs=in_specs,
        out_specs=[
            pl.BlockSpec((bs, bw), row),
            c_spec,
            pl.BlockSpec((bs, nh, dk), lambda i: (i, 0, 0)),
            pl.BlockSpec((bs, nh), row),
        ],
        out_shape=[
            jax.ShapeDtypeStruct((nd, bw), BF16),
            jax.ShapeDtypeStruct(c_all.shape, F32),
            jax.ShapeDtypeStruct(n.shape, F32),
            jax.ShapeDtypeStruct(m_state.shape, F32),
        ],
        input_output_aliases=aliases,
        name="mlstm_decode",
        compiler_params=_params("parallel"),
    )(*args)


def _split3(x):
    hi = x.astype(BF16).astype(F32)
    mid = (x - hi).astype(BF16).astype(F32)
    lo = (x - hi - mid).astype(BF16).astype(F32)
    return hi, mid, lo


def _fgate_kernel(z_ref, b_ref, lf_ref, qa_ref, ka_ref, *, nh, gate_col):
    lf = _log_sigmoid(z_ref[...] + b_ref[...])
    lf_ref[...] = lf
    t = lf.shape[0]
    row = lax.broadcasted_iota(jnp.int32, lf.shape, 0)
    lane = lax.broadcasted_iota(jnp.int32, lf.shape, 1)
    acc = lf
    s = 1
    while s < t:
        acc = acc + jnp.where(row >= s, pltpu.roll(acc, s, 0), 0.0)
        s *= 2
    acc = acc * LOG2E
    for h in range(nh):
        hi, mid, lo = _split3(acc[:, gate_col + h:gate_col + h + 1])
        terms = jnp.where(lane % 3 == 0, hi, jnp.where(lane % 3 == 1, mid, lo))
        sl = slice(h * LANES, (h + 1) * LANES)
        qa_ref[:, sl] = jnp.where(lane < 3, terms, jnp.where(lane < 6, 1.0, 0.0)).astype(BF16)
        ka_ref[:, sl] = jnp.where(lane < 3, 1.0, jnp.where(lane < 6, -terms, 0.0)).astype(BF16)


def _fgate(z, bias, n_seq, nh, gate_col):
    m = z.shape[0]
    t = m // n_seq
    spec = pl.BlockSpec((t, LANES), lambda n: (n, 0))
    aug = pl.BlockSpec((t, nh * LANES), lambda n: (n, 0))
    return pl.pallas_call(
        functools.partial(_fgate_kernel, nh=nh, gate_col=gate_col),
        grid=(n_seq,),
        in_specs=[spec, pl.BlockSpec((1, LANES), lambda n: (0, 0))],
        out_specs=[spec, aug, aug],
        out_shape=[jax.ShapeDtypeStruct((m, LANES), F32)]
                  + [jax.ShapeDtypeStruct((m, nh * LANES), BF16)] * 2,
        name="fox_gate_cumsum",
        compiler_params=_params("parallel"),
    )(z, bias)


def _attn_kernel(q_ref, qa_ref, k_ref, ka_ref, v_ref, o_ref, m_scr, l_scr, acc_scr, *, nh, dh):
    i, j = pl.program_id(1), pl.program_id(2)
    bq, bk = q_ref.shape[0], k_ref.shape[0]
    nrep = bk // LANES

    @pl.when(j == 0)
    def _():
        m_scr[...] = jnp.full_like(m_scr, -jnp.inf)
        l_scr[...] = jnp.zeros_like(l_scr)
        acc_scr[...] = jnp.zeros_like(acc_scr)

    def update(on_diagonal):
        if on_diagonal:
            mask = (lax.broadcasted_iota(jnp.int32, (bq, bk), 0)
                    >= lax.broadcasted_iota(jnp.int32, (bq, bk), 1))
        ones = jnp.ones((bk, LANES), BF16)
        for h in range(nh):
            sl = slice(h * dh, (h + 1) * dh)
            s = lax.dot_general(jnp.concatenate([q_ref[:, sl], qa_ref[:, sl]], axis=1),
                                jnp.concatenate([k_ref[:, sl], ka_ref[:, sl]], axis=1),
                                NT_DIMS, preferred_element_type=F32)
            if on_diagonal:
                s = jnp.where(mask, s, -jnp.inf)
            m_prev = m_scr[h]
            m_new = jnp.maximum(m_prev, jnp.max(s, axis=1, keepdims=True))
            p = jnp.exp2(s - jnp.concatenate([m_new] * nrep, axis=1))
            alpha = jnp.exp2(m_prev - m_new)
            pv = _dot(p.astype(BF16), jnp.concatenate([v_ref[:, sl], ones], axis=1))
            l_scr[h] = alpha * l_scr[h] + pv[:, dh:]
            acc_scr[:, sl] = alpha * acc_scr[:, sl] + pv[:, :dh]
            m_scr[h] = m_new

    @pl.when(j < i)
    def _():
        update(False)

    @pl.when(j == i)
    def _():
        update(True)

    @pl.when(j == pl.num_programs(2) - 1)
    def _():
        for h in range(nh):
            sl = slice(h * dh, (h + 1) * dh)
            o_ref[:, sl] = (acc_scr[:, sl] / l_scr[h]).astype(BF16)


def _attn(q, qa, k, ka, v, n_seq, nh, dh, blk_pref=512):
    m, c = q.shape
    t = m // n_seq
    assert dh == LANES, "running softmax stats and the bias columns are one head-dim wide"
    bq = bk = _blk(t, blk_pref)
    nq = nk = t // bq
    q_spec = pl.BlockSpec((bq, c), lambda b, i, j: (b * nq + i, 0))
    kv_spec = pl.BlockSpec((bk, c), lambda b, i, j: (b * nk + jnp.minimum(j, i), 0))
    return pl.pallas_call(
        functools.partial(_attn_kernel, nh=nh, dh=dh),
        grid=(n_seq, nq, nk),
        in_specs=[q_spec, q_spec, kv_spec, kv_spec, kv_spec],
        out_specs=q_spec,
        out_shape=jax.ShapeDtypeStruct((m, c), BF16),
        scratch_shapes=[pltpu.VMEM((nh, bq, LANES), F32), pltpu.VMEM((nh, bq, LANES), F32),
                        pltpu.VMEM((bq, c), F32)],
        name="fox_prompt_attn",
        compiler_params=_params("parallel", "parallel", "arbitrary"),
    )(q, qa, k, ka, v)


def _decode_attn_kernel(pt_ref, q_ref, kn_ref, vn_ref, z_ref, b_ref, *rest,
                        nh, dh, gate_col, pps):
    k_refs, v_refs, lf_refs = rest[:pps], rest[pps:2 * pps], rest[2 * pps:3 * pps]
    o_ref, lfn_ref, lf_scr, m_scr, l_scr, acc_scr, carry_scr = rest[3 * pps:]
    step = pl.program_id(1)
    page = k_refs[0].shape[0]
    w = page * nh
    q = q_ref[0]
    qb = q.astype(BF16)
    lf_new = _log_sigmoid(z_ref[0] + b_ref[...])
    lfn_ref[0] = lf_new
    pick = (lax.broadcasted_iota(jnp.int32, (nh, LANES), 1)
            == lax.broadcasted_iota(jnp.int32, (nh, LANES), 0) + gate_col)
    fn_col = jnp.sum(jnp.where(pick, jnp.broadcast_to(lf_new, (nh, LANES)), 0.0),
                     axis=1, keepdims=True) * LOG2E
    own_head = (lax.broadcasted_iota(jnp.int32, (nh, w), 1) % nh
                == lax.broadcasted_iota(jnp.int32, (nh, w), 0))

    @pl.when(step == 0)
    def _():
        m_scr[...] = jnp.full_like(m_scr, -jnp.inf)
        l_scr[...] = jnp.zeros_like(l_scr)
        acc_scr[...] = jnp.zeros_like(acc_scr)
        carry_scr[...] = jnp.zeros_like(carry_scr)

    for r in range(pps):
        lf_scr[r:r + 1, :] = lf_refs[r][...]
    lf = lf_scr[...]
    pos = lax.broadcasted_iota(jnp.int32, lf.shape, 1)
    rowi = lax.broadcasted_iota(jnp.int32, lf.shape, 0)
    suf, tot = lf, lf
    sh = nh
    while sh < w:
        suf = suf + jnp.where(pos < w - sh, pltpu.roll(suf, w - sh, 1), 0.0)
        tot = tot + pltpu.roll(tot, sh, 1)
        sh *= 2
    newer = tot
    sh = 1
    while sh < pps:
        newer = newer + jnp.where(rowi >= sh, pltpu.roll(newer, sh, 0), 0.0)
        sh *= 2
    bias = (suf - lf + newer - tot + carry_scr[...]) * LOG2E

    scores = []
    for r in range(pps):
        k2 = k_refs[r][...].reshape(w, dh).astype(BF16)
        s = lax.dot_general(qb, k2, NT_DIMS, preferred_element_type=F32)
        scores.append(jnp.where(own_head, s + fn_col + bias[r:r + 1, :], -jnp.inf))
    m_prev = m_scr[...]
    m_new = jnp.maximum(m_prev, jnp.max(functools.reduce(jnp.maximum, scores),
                                        axis=1, keepdims=True))
    alpha = jnp.exp2(m_prev - m_new)
    l_run = alpha * l_scr[...]
    acc = alpha * acc_scr[...]
    for r in range(pps):
        pe = jnp.exp2(scores[r] - m_new)
        l_run = l_run + jnp.sum(pe, axis=1, keepdims=True)
        acc = acc + _dot(pe.astype(BF16), v_refs[r][...].reshape(w, dh).astype(BF16))
    m_scr[...] = m_new
    l_scr[...] = l_run
    acc_scr[...] = acc
    carry_scr[...] = carry_scr[...] + newer[pps - 1:pps, :]

    @pl.when(step == pl.num_programs(1) - 1)
    def _():
        s_new = jnp.sum(q * kn_ref[0], axis=1, keepdims=True)
        m_fin = jnp.maximum(m_new, s_new)
        al = jnp.exp2(m_new - m_fin)
        pn = jnp.exp2(s_new - m_fin)
        o_ref[0] = ((al * acc + pn * vn_ref[0]) / (al * l_run + pn)).astype(o_ref.dtype)


def _decode_attn(page_table, q, k_new, v_new, z, bias, cache_k, cache_v, cache_lf,
                 layer, gate_col, pps_pref=16):
    nd, nh, dh = q.shape
    n_pages = page_table.shape[1]
    page = cache_k.shape[2]
    w = page * nh
    assert (page & (page - 1)) == 0, "per-head scans over a page use power-of-two strides"
    pps = _blk(n_pages, pps_pref)
    assert (pps & (pps - 1)) == 0
    row3 = lambda n, p, pt: (n, 0, 0)

    def page_map(r, tail):
        return lambda n, p, pt: (layer, pt[n, n_pages - 1 - (p * pps + r)]) + tail

    kv_specs = [pl.BlockSpec((None, None, page, nh, dh), page_map(r, (0, 0, 0))) for r in range(pps)]
    lf_specs = [pl.BlockSpec((None, None, 1, w), page_map(r, (0, 0))) for r in range(pps)]
    grid_spec = pltpu.PrefetchScalarGridSpec(
        num_scalar_prefetch=1,
        grid=(nd, n_pages // pps),
        in_specs=[
            pl.BlockSpec((1, nh, dh), row3),
            pl.BlockSpec((1, nh, dh), row3),
            pl.BlockSpec((1, nh, dh), row3),
            pl.BlockSpec((1, 1, LANES), row3),
            pl.BlockSpec((1, LANES), lambda n, p, pt: (0, 0)),
        ] + kv_specs + kv_specs + lf_specs,
        out_specs=[pl.BlockSpec((1, nh, dh), row3), pl.BlockSpec((1, 1, LANES), row3)],
        scratch_shapes=[pltpu.VMEM((pps, w), F32), pltpu.VMEM((nh, 1), F32), pltpu.VMEM((nh, 1), F32),
                        pltpu.VMEM((nh, dh), F32), pltpu.VMEM((1, w), F32)],
    )
    out, lf_new = pl.pallas_call(
        functools.partial(_decode_attn_kernel, nh=nh, dh=dh, gate_col=gate_col, pps=pps),
        grid_spec=grid_spec,
        out_shape=[jax.ShapeDtypeStruct((nd, nh, dh), F32),
                   jax.ShapeDtypeStruct((nd, 1, LANES), F32)],
        name="fox_decode_attn",
        compiler_params=_params("parallel", "arbitrary"),
    )(page_table, q, k_new, v_new, z.reshape(nd, 1, LANES), bias,
      *([cache_k] * pps), *([cache_v] * pps), *([cache_lf] * pps))
    return out, lf_new.reshape(nd, LANES)


def _merge_kernel(h_ref, yc_ref, ym_ref, yf_ref, wg_ref, wb_ref, o_ref, acc_scr):
    b = pl.program_id(2)

    @pl.when(b == 0)
    def _():
        acc_scr[...] = jnp.zeros_like(acc_scr)

    y = jnp.where(b == 0, yc_ref[...], jnp.where(b == 1, ym_ref[...], yf_ref[...]))
    gate = jax.nn.sigmoid(_dot(h_ref[...], wg_ref[...]))
    acc = acc_scr[...] + gate * _dot(y, wb_ref[...])
    acc_scr[...] = acc
    o_ref[...] = acc.astype(BF16)


def _merge(h, yc, ym, yf, w_gate, w_branch, bm_pref=1024, bn_pref=1024):
    m, d = h.shape
    bw = yc.shape[1]
    bm, bn = _blk(m, bm_pref), _blk(d, bn_pref)
    nc = d // bn
    y_spec = pl.BlockSpec((bm, bw), lambda i, c, b: (i, 0))
    return pl.pallas_call(
        _merge_kernel,
        grid=(m // bm, nc, 3),
        in_specs=[
            pl.BlockSpec((bm, d), lambda i, c, b: (i, 0)),
            y_spec, y_spec, y_spec,
            pl.BlockSpec((d, bn), lambda i, c, b: (0, b * nc + c)),
            pl.BlockSpec((None, bw, bn), lambda i, c, b: (b, 0, c)),
        ],
        out_specs=pl.BlockSpec((bm, bn), lambda i, c, b: (i, c)),
        out_shape=jax.ShapeDtypeStruct((m, d), BF16),
        scratch_shapes=[pltpu.VMEM((bm, bn), F32)],
        name="gated_merge",
        compiler_params=_params("parallel", "parallel", "arbitrary"),
    )(h, yc, ym, yf, w_gate, w_branch)


def kernel(x_prompt, x_sample, cache_k, cache_v, cache_logf, page_table, state_conv, state_C, state_n, state_m, norm_ffn1, ffn1_gate, ffn1_up, ffn1_down, norm_mix, w_in, conv_w, mlstm_b_i, mlstm_b_f, mlstm_norm, fox_b_f, fox_q_norm, fox_k_norm, w_branch, w_out, norm_ffn2, ffn2_gate, ffn2_up, ffn2_down):
    depth = w_in.shape[0]
    n_pr, t_pr, d = x_prompt.shape
    n_dec = x_sample.shape[0]
    conv_c = conv_w.shape[2]
    nh_m, dk, dv = state_C.shape[2:]
    n_pool, page, nh_f, dh = cache_k.shape[1:]
    fox_c = nh_f * dh
    assert x_sample.shape[1] == 1, "sample group is one new token per sequence"
    assert 2 * nh_m + nh_f <= LANES

    o_conv = 0
    o_mq = 3 * conv_c
    o_mo = o_mq + 2 * nh_m * dk + nh_m * dv
    o_mi = o_mo + nh_m * dv
    o_fq = o_mi + 2 * nh_m
    o_ff = o_fq + 3 * fox_c
    o_gz = o_ff + nh_f
    gate_col = 2 * nh_m

    xp = x_prompt.reshape(n_pr * t_pr, d)
    xs = x_sample.reshape(n_dec, d)
    clf = cache_logf.reshape(depth, n_pool, 1, page * nh_f)

    mqkv_scale = jnp.concatenate([jnp.ones((nh_m * dk,), F32),
                                  jnp.full((nh_m * dk,), dk ** -0.5, F32),
                                  jnp.ones((nh_m * dv,), F32)])[None, :]
    outs = {k: [] for k in ("lp", "ks", "vs", "ls", "cbp", "cbs", "cp", "np", "mp", "ns", "ms")}
    kv_stacks = None
    c_stack = None
    for l in range(depth):
        bf = lambda a: a.astype(BF16)
        wi = w_in[l]
        w_conv = bf(wi[:, o_conv:o_mq])
        w_mqkv = bf(wi[:, o_mq:o_mo])
        w_mo = bf(wi[:, o_mo:o_mi])
        w_small = bf(jnp.concatenate(
            [wi[:, o_mi:o_fq], wi[:, o_ff:o_gz],
             jnp.zeros((d, LANES - 2 * nh_m - nh_f), F32)], axis=1))
        w_fox = bf(wi[:, o_fq:o_ff])
        w_gate = bf(wi[:, o_gz:])
        w_br, w_o = bf(w_branch[l]), bf(w_out[l])
        n1, nm, n2 = norm_ffn1[l][None, :], norm_mix[l][None, :], norm_ffn2[l][None, :]
        small_bias = jnp.concatenate(
            [mlstm_b_i[l], mlstm_b_f[l], fox_b_f[l],
             jnp.zeros((LANES - 2 * nh_m - nh_f,), F32)])[None, :]
        m_norm = mlstm_norm[l][None, :]
        qn, kn = fox_q_norm[l][None, :], fox_k_norm[l][None, :]
        cw = conv_w[l]

        xs, hs, *w_ffn1 = _ffn(xs, n1, (ffn1_gate, ffn1_up, ffn1_down), nm, True, layer=l)
        yc_s, u_s = _conv_decode(hs, w_conv, cw, state_conv[l, :, 0, :], state_conv[l, :, 1, :])
        qkv_s = _mm(hs, w_mqkv, BF16, scale=mqkv_scale)
        og_s = _mm(hs, w_mo, BF16, act="sigmoid")
        z_s = _mm(hs, w_small, F32)
        q_s, _, _, kf_s, vf_s = _fox_in(hs, w_fox, qn, kn, dh, 0, 1)
        ym_s, c_stack, n_s, m_s = _mlstm_decode(qkv_s, og_s, z_s, small_bias, m_norm,
                                                state_C, state_n[l], state_m[l], l, c_stack)
        heads = lambda a: a.astype(F32).reshape(n_dec, nh_f, dh)
        yf_s, lf_s = _decode_attn(page_table, heads(q_s), heads(kf_s), heads(vf_s), z_s, small_bias,
                                  cache_k, cache_v, clf, l, gate_col)
        yf_s = yf_s.reshape(n_dec, fox_c).astype(BF16)
        mg_s = _merge(hs, yc_s, ym_s, yf_s, w_gate, w_br)
        xs = _mm(mg_s, w_o, F32, res=xs)
        xs, *w_ffn2 = _ffn(xs, n2, (ffn2_gate, ffn2_up, ffn2_down), n2, False, layer=l)

        xp, hp = _ffn(xp, n1, w_ffn1, nm, True)
        yc_p, tail_p = _conv_prompt(hp, w_conv, cw, n_pr)
        qkv_p = _mm(hp, w_mqkv, BF16, scale=mqkv_scale)
        og_p = _mm(hp, w_mo, BF16, act="sigmoid")
        z_p = _mm(hp, w_small, F32)
        q_p, kb_p, vb_p, *kv_stacks = _fox_in(hp, w_fox, qn, kn, dh, l, depth, kv_stacks)
        ym_p, c_p, n_p, m_p = _mlstm_prompt(qkv_p, og_p, z_p, small_bias, m_norm, n_pr, nh_m, dk, dv)
        lf_p, qa_p, ka_p = _fgate(z_p, small_bias, n_pr, nh_f, gate_col)
        yf_p = _attn(q_p, qa_p, kb_p, ka_p, vb_p, n_pr, nh_f, dh)
        mg_p = _merge(hp, yc_p, ym_p, yf_p, w_gate, w_br)
        xp = _mm(mg_p, w_o, F32, res=xp)
        (xp,) = _ffn(xp, n2, w_ffn2, n2, False)

        outs["lp"].append(lf_p[:, gate_col:gate_col + nh_f].reshape(n_pr, t_pr, nh_f))
        outs["ks"].append(kf_s.reshape(n_dec, 1, nh_f, dh))
        outs["vs"].append(vf_s.reshape(n_dec, 1, nh_f, dh))
        outs["ls"].append(lf_s[:, gate_col:gate_col + nh_f].reshape(n_dec, 1, nh_f))
        outs["cbp"].append(tail_p)
        outs["cbs"].append(jnp.stack([state_conv[l, :, 1, :], u_s], axis=1))
        outs["cp"].append(c_p)
        outs["np"].append(n_p)
        outs["mp"].append(m_p[:, :nh_m, 0])
        outs["ns"].append(n_s)
        outs["ms"].append(m_s)

    st = {k: jnp.stack(v) for k, v in outs.items()}
    k_prompt, v_prompt = (a.reshape(depth, n_pr, t_pr, nh_f, dh) for a in kv_stacks)
    return (xp.reshape(n_pr, t_pr, d), xs.reshape(n_dec, 1, d),
            k_prompt, v_prompt, st["lp"], st["ks"], st["vs"], st["ls"],
            st["cbp"], st["cbs"], st["cp"], st["np"], st["mp"], c_stack, st["ns"], st["ms"])
```

```python
import functools

import jax
import jax.numpy as jnp
from jax import lax
from jax.experimental import pallas as pl
from jax.experimental.pallas import tpu as pltpu

F32 = jnp.float32
BF16 = jnp.bfloat16
EPS = 1e-6
LANES = 128
VMEM_LIMIT = 56 * 1024 * 1024
M_CHUNK = 512
NT_DIMS = (((1,), (1,)), ((), ()))
LOG2E = 1.4426950408889634


def _params(*sem):
    return pltpu.CompilerParams(dimension_semantics=sem, vmem_limit_bytes=VMEM_LIMIT)


def _blk(n, pref):
    if n <= pref:
        return n
    b = pref
    while n % b:
        b //= 2
    return b


def _rms(x, w):
    return x * lax.rsqrt(jnp.mean(x * x, axis=-1, keepdims=True) + EPS) * w


def _dot(a, b):
    return jnp.dot(a, b, preferred_element_type=F32)


def _log_sigmoid(x):
    return jnp.minimum(x, 0.0) - jnp.log(1.0 + jnp.exp(-jnp.abs(x)))


FFN_BF = 512


def _ffn_kernel(x_ref, n_ref, *refs, emit_next, from_f32):
    refs = list(refs)
    w_refs = [refs.pop(0) for _ in range(3 if from_f32 else 2)]
    nn_ref, o_ref = refs.pop(0), refs.pop(0)
    hn_ref = refs.pop(0) if emit_next else None
    w_out_refs = [refs.pop(0), refs.pop(0)] if from_f32 else None
    (h_scr,) = refs
    j = pl.program_id(1)

    @pl.when(j == 0)
    def _():
        x = x_ref[...]
        h_scr[...] = _rms(x, n_ref[...]).astype(BF16)
        o_ref[...] = x

    if from_f32:
        wgu = jnp.concatenate([w_refs[0][...].astype(BF16), w_refs[1][...].astype(BF16)], axis=1)
        wd = w_refs[2][...].astype(BF16)
        w_out_refs[0][...] = wgu
        w_out_refs[1][...] = wd
    else:
        wgu, wd = w_refs[0][...], w_refs[1][...]
    bf = wd.shape[0]
    gu = _dot(h_scr[...], wgu)
    g, u = gu[:, :bf], gu[:, bf:]
    a = (0.5 * g * jax.nn.sigmoid(g) * u).astype(BF16)
    o_ref[...] += _dot(a, wd)

    if emit_next:
        @pl.when(j == pl.num_programs(1) - 1)
        def _():
            hn_ref[...] = _rms(o_ref[...], nn_ref[...]).astype(BF16)


def _ffn(x, norm, weights, next_norm, emit_next, layer=None, bm_pref=512):
    m, d = x.shape
    from_f32 = layer is not None
    f = weights[-1].shape[-2]
    bm, bf = _blk(m, bm_pref), _blk(f, FFN_BF)
    row = pl.BlockSpec((bm, d), lambda i, j: (i, 0))
    vec = pl.BlockSpec((1, d), lambda i, j: (0, 0))
    fused_specs = [pl.BlockSpec((d, 2 * bf), lambda i, j: (0, j)),
                   pl.BlockSpec((bf, d), lambda i, j: (j, 0))]
    out_shape = [jax.ShapeDtypeStruct((m, d), F32)]
    out_specs = [row]
    if emit_next:
        out_shape.append(jax.ShapeDtypeStruct((m, d), BF16))
        out_specs.append(row)
    if from_f32:
        assert m == bm, "each weight block must be visited exactly once"
        w_specs = [pl.BlockSpec((None, d, bf), lambda i, j: (layer, 0, j)),
                   pl.BlockSpec((None, d, bf), lambda i, j: (layer, 0, j)),
                   pl.BlockSpec((None, bf, d), lambda i, j: (layer, j, 0))]
        out_shape += [jax.ShapeDtypeStruct((d, 2 * f), BF16), jax.ShapeDtypeStruct((f, d), BF16)]
        out_specs += fused_specs
    else:
        w_specs = fused_specs
    return pl.pallas_call(
        functools.partial(_ffn_kernel, emit_next=emit_next, from_f32=from_f32),
        grid=(m // bm, f // bf),
        in_specs=[row, vec] + w_specs + [vec],
        out_specs=out_specs,
        out_shape=out_shape,
        scratch_shapes=[pltpu.VMEM((bm, d), BF16)],
        name="swiglu_block",
        compiler_params=_params("parallel", "arbitrary"),
    )(x, norm, *weights, next_norm)


def _mm_kernel(*refs, act, has_scale, has_res):
    h_ref, w_ref = refs[0], refs[1]
    k = 2
    acc = _dot(h_ref[...], w_ref[...])
    if has_scale:
        acc = acc * refs[k][...]
        k += 1
    if act == "sigmoid":
        acc = jax.nn.sigmoid(acc)
    if has_res:
        acc = refs[k][...] + acc
        k += 1
    o_ref = refs[k]
    o_ref[...] = acc.astype(o_ref.dtype)


def _mm(h, w, out_dtype, act=None, scale=None, res=None, bm_pref=1024, bn_pref=1024):
    m, kd = h.shape
    n = w.shape[1]
    bm, bn = _blk(m, bm_pref), _blk(n, bn_pref)
    args = [h, w]
    in_specs = [pl.BlockSpec((bm, kd), lambda j, i: (i, 0)),
                pl.BlockSpec((kd, bn), lambda j, i: (0, j))]
    if scale is not None:
        args.append(scale)
        in_specs.append(pl.BlockSpec((1, bn), lambda j, i: (0, j)))
    if res is not None:
        args.append(res)
        in_specs.append(pl.BlockSpec((bm, bn), lambda j, i: (i, j)))
    return pl.pallas_call(
        functools.partial(_mm_kernel, act=act, has_scale=scale is not None, has_res=res is not None),
        grid=(n // bn, m // bm),
        in_specs=in_specs,
        out_specs=pl.BlockSpec((bm, bn), lambda j, i: (i, j)),
        out_shape=jax.ShapeDtypeStruct((m, n), out_dtype),
        name="proj_" + (act or "lin") + ("_res" if res is not None else ""),
        compiler_params=_params("parallel", "parallel"),
    )(*args)


def _conv_prompt_kernel(h_ref, wu_ref, wb_ref, wc_ref, cw_ref, y_ref, tail_ref, carry_scr,
                        *, blocks_per_seq):
    i = pl.program_id(1)

    @pl.when(i % blocks_per_seq == 0)
    def _():
        carry_scr[...] = jnp.zeros_like(carry_scr)

    h = h_ref[...]
    u = _dot(h, wc_ref[...]) * _dot(h, wu_ref[...])
    cb = _dot(h, wb_ref[...])
    bm = u.shape[0]
    row = lax.broadcasted_iota(jnp.int32, u.shape, 0)
    c1 = carry_scr[7:8, :]
    c2 = carry_scr[6:7, :]
    u1 = jnp.where(row >= 1, pltpu.roll(u, 1, 0), c1)
    u2 = jnp.where(row >= 2, pltpu.roll(u, 2, 0), jnp.where(row == 0, c2, c1))
    cw = cw_ref[...]
    y = cb * (cw[0:1, :] * u2 + cw[1:2, :] * u1 + cw[2:3, :] * u)
    y_ref[...] = y.astype(BF16)
    carry_scr[...] = u[bm - 8:, :]
    tail_ref[0] = u[bm - 2:, :]


def _conv_prompt(h, w_conv, conv_w, n_seq, bm_pref=1024, bn_pref=512):
    m, kd = h.shape
    c = conv_w.shape[1]
    t = m // n_seq
    bm, bn = _blk(t, bm_pref), _blk(c, bn_pref)
    nb = c // bn
    bps = t // bm
    return pl.pallas_call(
        functools.partial(_conv_prompt_kernel, blocks_per_seq=bps),
        grid=(nb, m // bm),
        in_specs=[
            pl.BlockSpec((bm, kd), lambda j, i: (i, 0)),
            pl.BlockSpec((kd, bn), lambda j, i: (0, j)),
            pl.BlockSpec((kd, bn), lambda j, i: (0, j + nb)),
            pl.BlockSpec((kd, bn), lambda j, i: (0, j + 2 * nb)),
            pl.BlockSpec((conv_w.shape[0], bn), lambda j, i: (0, j)),
        ],
        out_specs=[
            pl.BlockSpec((bm, bn), lambda j, i: (i, j)),
            pl.BlockSpec((1, 2, bn), lambda j, i: (i // bps, 0, j)),
        ],
        out_shape=[jax.ShapeDtypeStruct((m, c), BF16),
                   jax.ShapeDtypeStruct((n_seq, 2, c), F32)],
        scratch_shapes=[pltpu.VMEM((8, bn), F32)],
        name="conv_prompt",
        compiler_params=_params("parallel", "arbitrary"),
    )(h, w_conv, w_conv, w_conv, conv_w)


def _conv_decode_kernel(h_ref, wu_ref, wb_ref, wc_ref, cw_ref, b0_ref, b1_ref, y_ref, u_ref):
    h = h_ref[...]
    u = _dot(h, wc_ref[...]) * _dot(h, wu_ref[...])
    cb = _dot(h, wb_ref[...])
    cw = cw_ref[...]
    y = cb * (cw[0:1, :] * b0_ref[...] + cw[1:2, :] * b1_ref[...] + cw[2:3, :] * u)
    y_ref[...] = y.astype(BF16)
    u_ref[...] = u


def _conv_decode(h, w_conv, conv_w, buf0, buf1, bn_pref=512):
    m, kd = h.shape
    c = conv_w.shape[1]
    bn = _blk(c, bn_pref)
    nb = c // bn
    return pl.pallas_call(
        _conv_decode_kernel,
        grid=(nb,),
        in_specs=[
            pl.BlockSpec((m, kd), lambda j: (0, 0)),
            pl.BlockSpec((kd, bn), lambda j: (0, j)),
            pl.BlockSpec((kd, bn), lambda j: (0, j + nb)),
            pl.BlockSpec((kd, bn), lambda j: (0, j + 2 * nb)),
            pl.BlockSpec((conv_w.shape[0], bn), lambda j: (0, j)),
            pl.BlockSpec((m, bn), lambda j: (0, j)),
            pl.BlockSpec((m, bn), lambda j: (0, j)),
        ],
        out_specs=[pl.BlockSpec((m, bn), lambda j: (0, j)),
                   pl.BlockSpec((m, bn), lambda j: (0, j))],
        out_shape=[jax.ShapeDtypeStruct((m, c), BF16), jax.ShapeDtypeStruct((m, c), F32)],
        name="conv_decode",
        compiler_params=_params("parallel"),
    )(h, w_conv, w_conv, w_conv, conv_w, buf0, buf1)


def _fox_in_kernel(h_ref, wq_ref, wk_ref, wv_ref, qn_ref, kn_ref, *rest, dh, q_scale):
    q_ref, kb_ref, vb_ref, kf_ref, vf_ref = rest[-5:]
    h = h_ref[...]
    fq = _dot(h, wq_ref[...])
    fk = _dot(h, wk_ref[...])
    fv = _dot(h, wv_ref[...])
    for hh in range(fq.shape[1] // dh):
        sl = slice(hh * dh, (hh + 1) * dh)
        qh = _rms(fq[:, sl], qn_ref[...])
        kh = _rms(fk[:, sl], kn_ref[...])
        q_ref[:, sl] = (qh * q_scale).astype(BF16)
        kf_ref[:, sl] = kh
        kb_ref[:, sl] = kh.astype(BF16)
    vf_ref[...] = fv
    vb_ref[...] = fv.astype(BF16)


def _fox_in(h, w_fox, qn, kn, dh, layer, depth, stacks=None, bm_pref=1024, bn_pref=512):
    m, kd = h.shape
    c = w_fox.shape[1] // 3
    bm, bn = _blk(m, bm_pref), _blk(c, bn_pref)
    nb = c // bn
    o_spec = pl.BlockSpec((bm, bn), lambda j, i: (i, j))
    s_spec = pl.BlockSpec((None, bm, bn), lambda j, i: (layer, i, j))
    args = [h, w_fox, w_fox, w_fox, qn, kn]
    in_specs = [
        pl.BlockSpec((bm, kd), lambda j, i: (i, 0)),
        pl.BlockSpec((kd, bn), lambda j, i: (0, j)),
        pl.BlockSpec((kd, bn), lambda j, i: (0, j + nb)),
        pl.BlockSpec((kd, bn), lambda j, i: (0, j + 2 * nb)),
        pl.BlockSpec((1, dh), lambda j, i: (0, 0)),
        pl.BlockSpec((1, dh), lambda j, i: (0, 0)),
    ]
    aliases = {}
    if stacks is not None:
        aliases = {len(args): 3, len(args) + 1: 4}
        args += list(stacks)
        in_specs += [pl.BlockSpec(memory_space=pl.ANY)] * 2
    return pl.pallas_call(
        functools.partial(_fox_in_kernel, dh=dh, q_scale=dh ** -0.5 * LOG2E),
        grid=(nb, m // bm),
        in_specs=in_specs,
        out_specs=[o_spec] * 3 + [s_spec] * 2,
        out_shape=[jax.ShapeDtypeStruct((m, c), BF16)] * 3
                  + [jax.ShapeDtypeStruct((depth, m, c), F32)] * 2,
        input_output_aliases=aliases,
        name="fox_qkv",
        compiler_params=_params("parallel", "parallel"),
    )(*args)


def _mlstm_prompt_kernel(qkv_ref, og_ref, z_ref, b_ref, mn_ref,
                         y_ref, c_ref, n_ref, m_ref, *, nh, dk, dv):
    @pl.when(pl.program_id(1) == 0)
    def _():
        c_ref[...] = jnp.zeros_like(c_ref)
        n_ref[...] = jnp.zeros_like(n_ref)
        m_ref[...] = jnp.zeros_like(m_ref)

    z = z_ref[...] + b_ref[...]
    L = z.shape[0]
    row = lax.broadcasted_iota(jnp.int32, z.shape, 0)
    lane = lax.broadcasted_iota(jnp.int32, z.shape, 1)
    bt = _log_sigmoid(z)
    s = 1
    while s < L:
        bt = bt + jnp.where(row >= s, pltpu.roll(bt, s, 0), 0.0)
        s *= 2
    g = jnp.where(lane < nh, z, bt)
    g_t = g.T
    causal = (lax.broadcasted_iota(jnp.int32, (L, L), 0)
              >= lax.broadcasted_iota(jnp.int32, (L, L), 1))
    for h in range(nh):
        q = qkv_ref[:, h * dk:(h + 1) * dk]
        k = qkv_ref[:, nh * dk + h * dk: nh * dk + (h + 1) * dk]
        v = qkv_ref[:, 2 * nh * dk + h * dv: 2 * nh * dk + (h + 1) * dv]
        ig_col, bt_col = g[:, h:h + 1], g[:, nh + h:nh + h + 1]
        ig_row, bt_row = g_t[h:h + 1, :], g_t[nh + h:nh + h + 1, :]
        m_prev = m_ref[0, h:h + 1, 0:1]
        dmat = jnp.where(causal, bt_col - bt_row + ig_row, -jnp.inf)
        inter = bt_col + m_prev
        m_t = jnp.maximum(inter, jnp.max(dmat, axis=1, keepdims=True))
        w = jnp.exp(dmat - m_t)
        a = jnp.exp(inter - m_t)
        sc = lax.dot_general(q, k, NT_DIMS, preferred_element_type=F32) * w
        c_old = c_ref[0, h]
        n_old = n_ref[0, h:h + 1, :]
        num = a * _dot(q, c_old.astype(BF16)) + _dot(sc.astype(BF16), v)
        den = (a * jnp.sum(q.astype(F32) * n_old, axis=1, keepdims=True)
               + jnp.sum(sc, axis=1, keepdims=True))
        hm = num / jnp.maximum(jnp.abs(den), jnp.exp(-m_t))
        m_last = m_t[L - 1:L, :]
        a_last = a[L - 1:L, :]
        w_last = jnp.exp(bt_col[L - 1:L, :] - bt_col + ig_col - m_last)
        kw = k.astype(F32) * w_last
        c_ref[0, h] = a_last * c_old + _dot(kw.T.astype(BF16), v)
        n_ref[0, h:h + 1, :] = a_last * n_old + jnp.sum(kw, axis=0, keepdims=True)
        m_ref[0, h:h + 1, :] = jnp.broadcast_to(m_last, (1, m_ref.shape[2]))
        vs = slice(h * dv, (h + 1) * dv)
        y_ref[:, vs] = (og_ref[:, vs].astype(F32) * _rms(hm, mn_ref[:, vs])).astype(BF16)


def _mlstm_prompt(qkv, og, z, bias, m_norm, n_seq, nh, dk, dv):
    m = qkv.shape[0]
    t = m // n_seq
    L = M_CHUNK if t % M_CHUNK == 0 else t
    nc = t // L
    bw = nh * dv
    return pl.pallas_call(
        functools.partial(_mlstm_prompt_kernel, nh=nh, dk=dk, dv=dv),
        grid=(n_seq, nc),
        in_specs=[
            pl.BlockSpec((L, qkv.shape[1]), lambda n, c: (n * nc + c, 0)),
            pl.BlockSpec((L, bw), lambda n, c: (n * nc + c, 0)),
            pl.BlockSpec((L, LANES), lambda n, c: (n * nc + c, 0)),
            pl.BlockSpec((1, LANES), lambda n, c: (0, 0)),
            pl.BlockSpec((1, bw), lambda n, c: (0, 0)),
        ],
        out_specs=[
            pl.BlockSpec((L, bw), lambda n, c: (n * nc + c, 0)),
            pl.BlockSpec((1, nh, dk, dv), lambda n, c: (n, 0, 0, 0)),
            pl.BlockSpec((1, nh, dk), lambda n, c: (n, 0, 0)),
            pl.BlockSpec((1, 8, LANES), lambda n, c: (n, 0, 0)),
        ],
        out_shape=[
            jax.ShapeDtypeStruct((m, bw), BF16),
            jax.ShapeDtypeStruct((n_seq, nh, dk, dv), F32),
            jax.ShapeDtypeStruct((n_seq, nh, dk), F32),
            jax.ShapeDtypeStruct((n_seq, 8, LANES), F32),
        ],
        name="mlstm_prompt",
        compiler_params=_params("parallel", "arbitrary"),
    )(qkv, og, z, bias, m_norm)


def _mlstm_decode_kernel(qkv_ref, og_ref, z_ref, b_ref, mn_ref, c_ref, n_ref, m_ref,
                         *rest, nh, dk, dv):
    y_ref, co_ref, no_ref, mo_ref = rest[-4:]
    bs = qkv_ref.shape[0]
    qkv = qkv_ref[...].astype(F32)
    og = og_ref[...].astype(F32)
    z = z_ref[...] + b_ref[...]
    lf_all = _log_sigmoid(z)
    eye = (lax.broadcasted_iota(jnp.int32, (dk, dk), 0)
           == lax.broadcasted_iota(jnp.int32, (dk, dk), 1))
    for s in range(bs):
        for h in range(nh):
            q = qkv[s:s + 1, h * dk:(h + 1) * dk]
            k = qkv[s:s + 1, nh * dk + h * dk: nh * dk + (h + 1) * dk]
            v = qkv[s:s + 1, 2 * nh * dk + h * dv: 2 * nh * dk + (h + 1) * dv]
            ig = z[s:s + 1, h:h + 1]
            lf = lf_all[s:s + 1, nh + h:nh + h + 1]
            m_old = m_ref[s:s + 1, h:h + 1]
            c_old = c_ref[s, h]
            n_old = n_ref[s, h:h + 1, :]
            inter = lf + m_old
            m_t = jnp.maximum(inter, ig)
            w = jnp.exp(ig - m_t)
            a = jnp.exp(inter - m_t)
            sc = jnp.sum(q * k, axis=1, keepdims=True) * w
            num = a * _dot(q.astype(BF16), c_old.astype(BF16)) + sc * v
            den = a * jnp.sum(q * n_old, axis=1, keepdims=True) + sc
            hm = num / jnp.maximum(jnp.abs(den), jnp.exp(-m_t))
            k_col = jnp.sum(jnp.where(eye, k, 0.0), axis=1, keepdims=True)
            co_ref[s, h] = a * c_old + (w * k_col) * v
            no_ref[s, h:h + 1, :] = a * n_old + w * k
            mo_ref[s:s + 1, h:h + 1] = m_t
            vs = slice(h * dv, (h + 1) * dv)
            y_ref[s:s + 1, vs] = (og[s:s + 1, vs] * _rms(hm, mn_ref[:, vs])).astype(BF16)


def _mlstm_decode(qkv, og, z, bias, m_norm, c_all, n, m_state, layer, c_stack=None):
    nd = qkv.shape[0]
    nh, dk, dv = c_all.shape[2:]
    bs = _blk(nd, 8)
    bw = nh * dv
    row = lambda i: (i, 0)
    c_spec = pl.BlockSpec((None, bs, nh, dk, dv), lambda i: (layer, i, 0, 0, 0))
    args = [qkv, og, z, bias, m_norm, c_all, n, m_state]
    in_specs = [
        pl.BlockSpec((bs, qkv.shape[1]), row),
        pl.BlockSpec((bs, bw), row),
        pl.BlockSpec((bs, LANES), row),
        pl.BlockSpec((1, LANES), lambda i: (0, 0)),
        pl.BlockSpec((1, bw), lambda i: (0, 0)),
        c_spec,
        pl.BlockSpec((bs, nh, dk), lambda i: (i, 0, 0)),
        pl.BlockSpec((bs, nh), row),
    ]
    aliases = {}
    if c_stack is not None:
        aliases = {len(args): 1}
        args.append(c_stack)
        in_specs.append(pl.BlockSpec(memory_space=pl.ANY))
    return pl.pallas_call(
        functools.partial(_mlstm_decode_kernel, nh=nh, dk=dk, dv=dv),
        grid=(nd // bs,),
        in_specs=in_specs,
        out_specs=[
            pl.BlockSpec((bs, bw), row),
            c_spec,
            pl.BlockSpec((bs, nh, dk), lambda i: (i, 0, 0)),
            pl.BlockSpec((bs, nh), row),
        ],
        out_shape=[
            jax.ShapeDtypeStruct((nd, bw), BF16),
            jax.ShapeDtypeStruct(c_all.shape, F32),
            jax.ShapeDtypeStruct(n.shape, F32),
            jax.ShapeDtypeStruct(m_state.shape, F32),
        ],
        input_output_aliases=aliases,
        name="mlstm_decode",
        compiler_params=_params("parallel"),
    )(*args)


def _split3(x):
    hi = x.astype(BF16).astype(F32)
    mid = (x - hi).astype(BF16).astype(F32)
    lo = (x - hi - mid).astype(BF16).astype(F32)
    return hi, mid, lo


def _fgate_kernel(z_ref, b_ref, lf_ref, qa_ref, ka_ref, *, nh, gate_col):
    lf = _log_sigmoid(z_ref[...] + b_ref[...])
    lf_ref[...] = lf
    t = lf.shape[0]
    row = lax.broadcasted_iota(jnp.int32, lf.shape, 0)
    lane = lax.broadcasted_iota(jnp.int32, lf.shape, 1)
    acc = lf
    s = 1
    while s < t:
        acc = acc + jnp.where(row >= s, pltpu.roll(acc, s, 0), 0.0)
        s *= 2
    acc = acc * LOG2E
    for h in range(nh):
        hi, mid, lo = _split3(acc[:, gate_col + h:gate_col + h + 1])
        terms = jnp.where(lane % 3 == 0, hi, jnp.where(lane % 3 == 1, mid, lo))
        sl = slice(h * LANES, (h + 1) * LANES)
        qa_ref[:, sl] = jnp.where(lane < 3, terms, jnp.where(lane < 6, 1.0, 0.0)).astype(BF16)
        ka_ref[:, sl] = jnp.where(lane < 3, 1.0, jnp.where(lane < 6, -terms, 0.0)).astype(BF16)


def _fgate(z, bias, n_seq, nh, gate_col):
    m = z.shape[0]
    t = m // n_seq
    spec = pl.BlockSpec((t, LANES), lambda n: (n, 0))
    aug = pl.BlockSpec((t, nh * LANES), lambda n: (n, 0))
    return pl.pallas_call(
        functools.partial(_fgate_kernel, nh=nh, gate_col=gate_col),
        grid=(n_seq,),
        in_specs=[spec, pl.BlockSpec((1, LANES), lambda n: (0, 0))],
        out_specs=[spec, aug, aug],
        out_shape=[jax.ShapeDtypeStruct((m, LANES), F32)]
                  + [jax.ShapeDtypeStruct((m, nh * LANES), BF16)] * 2,
        name="fox_gate_cumsum",
        compiler_params=_params("parallel"),
    )(z, bias)


def _attn_kernel(q_ref, qa_ref, k_ref, ka_ref, v_ref, o_ref, m_scr, l_scr, acc_scr, *, nh, dh):
    i, j = pl.program_id(1), pl.program_id(2)
    bq, bk = q_ref.shape[0], k_ref.shape[0]
    nrep = bk // LANES

    @pl.when(j == 0)
    def _():
        m_scr[...] = jnp.full_like(m_scr, -jnp.inf)
        l_scr[...] = jnp.zeros_like(l_scr)
        acc_scr[...] = jnp.zeros_like(acc_scr)

    def update(on_diagonal):
        if on_diagonal:
            mask = (lax.broadcasted_iota(jnp.int32, (bq, bk), 0)
                    >= lax.broadcasted_iota(jnp.int32, (bq, bk), 1))
        ones = jnp.ones((bk, LANES), BF16)
        for h in range(nh):
            sl = slice(h * dh, (h + 1) * dh)
            s = lax.dot_general(jnp.concatenate([q_ref[:, sl], qa_ref[:, sl]], axis=1),
                                jnp.concatenate([k_ref[:, sl], ka_ref[:, sl]], axis=1),
                                NT_DIMS, preferred_element_type=F32)
            if on_diagonal:
                s = jnp.where(mask, s, -jnp.inf)
            m_prev = m_scr[h]
            m_new = jnp.maximum(m_prev, jnp.max(s, axis=1, keepdims=True))
            p = jnp.exp2(s - jnp.concatenate([m_new] * nrep, axis=1))
            alpha = jnp.exp2(m_prev - m_new)
            pv = _dot(p.astype(BF16), jnp.concatenate([v_ref[:, sl], ones], axis=1))
            l_scr[h] = alpha * l_scr[h] + pv[:, dh:]
            acc_scr[:, sl] = alpha * acc_scr[:, sl] + pv[:, :dh]
            m_scr[h] = m_new

    @pl.when(j < i)
    def _():
        update(False)

    @pl.when(j == i)
    def _():
        update(True)

    @pl.when(j == pl.num_programs(2) - 1)
    def _():
        for h in range(nh):
            sl = slice(h * dh, (h + 1) * dh)
            o_ref[:, sl] = (acc_scr[:, sl] / l_scr[h]).astype(BF16)


def _attn(q, qa, k, ka, v, n_seq, nh, dh, blk_pref=512):
    m, c = q.shape
    t = m // n_seq
    assert dh == LANES, "running softmax stats and the bias columns are one head-dim wide"
    bq = bk = _blk(t, blk_pref)
    nq = nk = t // bq
    q_spec = pl.BlockSpec((bq, c), lambda b, i, j: (b * nq + i, 0))
    kv_spec = pl.BlockSpec((bk, c), lambda b, i, j: (b * nk + jnp.minimum(j, i), 0))
    return pl.pallas_call(
        functools.partial(_attn_kernel, nh=nh, dh=dh),
        grid=(n_seq, nq, nk),
        in_specs=[q_spec, q_spec, kv_spec, kv_spec, kv_spec],
        out_specs=q_spec,
        out_shape=jax.ShapeDtypeStruct((m, c), BF16),
        scratch_shapes=[pltpu.VMEM((nh, bq, LANES), F32), pltpu.VMEM((nh, bq, LANES), F32),
                        pltpu.VMEM((bq, c), F32)],
        name="fox_prompt_attn",
        compiler_params=_params("parallel", "parallel", "arbitrary"),
    )(q, qa, k, ka, v)


def _decode_attn_kernel(pt_ref, q_ref, kn_ref, vn_ref, z_ref, b_ref, *rest,
                        nh, dh, gate_col, pps):
    k_refs, v_refs, lf_refs = rest[:pps], rest[pps:2 * pps], rest[2 * pps:3 * pps]
    o_ref, lfn_ref, lf_scr, m_scr, l_scr, acc_scr, carry_scr = rest[3 * pps:]
    step = pl.program_id(1)
    page = k_refs[0].shape[0]
    w = page * nh
    q = q_ref[0]
    qb = q.astype(BF16)
    lf_new = _log_sigmoid(z_ref[0] + b_ref[...])
    lfn_ref[0] = lf_new
    pick = (lax.broadcasted_iota(jnp.int32, (nh, LANES), 1)
            == lax.broadcasted_iota(jnp.int32, (nh, LANES), 0) + gate_col)
    fn_col = jnp.sum(jnp.where(pick, jnp.broadcast_to(lf_new, (nh, LANES)), 0.0),
                     axis=1, keepdims=True) * LOG2E
    own_head = (lax.broadcasted_iota(jnp.int32, (nh, w), 1) % nh
                == lax.broadcasted_iota(jnp.int32, (nh, w), 0))

    @pl.when(step == 0)
    def _():
        m_scr[...] = jnp.full_like(m_scr, -jnp.inf)
        l_scr[...] = jnp.zeros_like(l_scr)
        acc_scr[...] = jnp.zeros_like(acc_scr)
        carry_scr[...] = jnp.zeros_like(carry_scr)

    for r in range(pps):
        lf_scr[r:r + 1, :] = lf_refs[r][...]
    lf = lf_scr[...]
    pos = lax.broadcasted_iota(jnp.int32, lf.shape, 1)
    rowi = lax.broadcasted_iota(jnp.int32, lf.shape, 0)
    suf, tot = lf, lf
    sh = nh
    while sh < w:
        suf = suf + jnp.where(pos < w - sh, pltpu.roll(suf, w - sh, 1), 0.0)
        tot = tot + pltpu.roll(tot, sh, 1)
        sh *= 2
    newer = tot
    sh = 1
    while sh < pps:
        newer = newer + jnp.where(rowi >= sh, pltpu.roll(newer, sh, 0), 0.0)
        sh *= 2
    bias = (suf - lf + newer - tot + carry_scr[...]) * LOG2E

    scores = []
    for r in range(pps):
        k2 = k_refs[r][...].reshape(w, dh).astype(BF16)
        s = lax.dot_general(qb, k2, NT_DIMS, preferred_element_type=F32)
        scores.append(jnp.where(own_head, s + fn_col + bias[r:r + 1, :], -jnp.inf))
    m_prev = m_scr[...]
    m_new = jnp.maximum(m_prev, jnp.max(functools.reduce(jnp.maximum, scores),
                                        axis=1, keepdims=True))
    alpha = jnp.exp2(m_prev - m_new)
    l_run = alpha * l_scr[...]
    acc = alpha * acc_scr[...]
    for r in range(pps):
        pe = jnp.exp2(scores[r] - m_new)
        l_run = l_run + jnp.sum(pe, axis=1, keepdims=True)
        acc = acc + _dot(pe.astype(BF16), v_refs[r][...].reshape(w, dh).astype(BF16))
    m_scr[...] = m_new
    l_scr[...] = l_run
    acc_scr[...] = acc
    carry_scr[...] = carry_scr[...] + newer[pps - 1:pps, :]

    @pl.when(step == pl.num_programs(1) - 1)
    def _():
        s_new = jnp.sum(q * kn_ref[0], axis=1, keepdims=True)
        m_fin = jnp.maximum(m_new, s_new)
        al = jnp.exp2(m_new - m_fin)
        pn = jnp.exp2(s_new - m_fin)
        o_ref[0] = ((al * acc + pn * vn_ref[0]) / (al * l_run + pn)).astype(o_ref.dtype)


def _decode_attn(page_table, q, k_new, v_new, z, bias, cache_k, cache_v, cache_lf,
                 layer, gate_col, pps_pref=16):
    nd, nh, dh = q.shape
    n_pages = page_table.shape[1]
    page = cache_k.shape[2]
    w = page * nh
    assert (page & (page - 1)) == 0, "per-head scans over a page use power-of-two strides"
    pps = _blk(n_pages, pps_pref)
    assert (pps & (pps - 1)) == 0
    row3 = lambda n, p, pt: (n, 0, 0)

    def page_map(r, tail):
        return lambda n, p, pt: (layer, pt[n, n_pages - 1 - (p * pps + r)]) + tail

    kv_specs = [pl.BlockSpec((None, None, page, nh, dh), page_map(r, (0, 0, 0))) for r in range(pps)]
    lf_specs = [pl.BlockSpec((None, None, 1, w), page_map(r, (0, 0))) for r in range(pps)]
    grid_spec = pltpu.PrefetchScalarGridSpec(
        num_scalar_prefetch=1,
        grid=(nd, n_pages // pps),
        in_specs=[
            pl.BlockSpec((1, nh, dh), row3),
            pl.BlockSpec((1, nh, dh), row3),
            pl.BlockSpec((1, nh, dh), row3),
            pl.BlockSpec((1, 1, LANES), row3),
            pl.BlockSpec((1, LANES), lambda n, p, pt: (0, 0)),
        ] + kv_specs + kv_specs + lf_specs,
        out_specs=[pl.BlockSpec((1, nh, dh), row3), pl.BlockSpec((1, 1, LANES), row3)],
        scratch_shapes=[pltpu.VMEM((pps, w), F32), pltpu.VMEM((nh, 1), F32), pltpu.VMEM((nh, 1), F32),
                        pltpu.VMEM((nh, dh), F32), pltpu.VMEM((1, w), F32)],
    )
    out, lf_new = pl.pallas_call(
        functools.partial(_decode_attn_kernel, nh=nh, dh=dh, gate_col=gate_col, pps=pps),
        grid_spec=grid_spec,
        out_shape=[jax.ShapeDtypeStruct((nd, nh, dh), F32),
                   jax.ShapeDtypeStruct((nd, 1, LANES), F32)],
        name="fox_decode_attn",
        compiler_params=_params("parallel", "arbitrary"),
    )(page_table, q, k_new, v_new, z.reshape(nd, 1, LANES), bias,
      *([cache_k] * pps), *([cache_v] * pps), *([cache_lf] * pps))
    return out, lf_new.reshape(nd, LANES)


def _merge_kernel(h_ref, yc_ref, ym_ref, yf_ref, wg_ref, wb_ref, o_ref, acc_scr):
    b = pl.program_id(2)

    @pl.when(b == 0)
    def _():
        acc_scr[...] = jnp.zeros_like(acc_scr)

    y = jnp.where(b == 0, yc_ref[...], jnp.where(b == 1, ym_ref[...], yf_ref[...]))
    gate = jax.nn.sigmoid(_dot(h_ref[...], wg_ref[...]))
    acc = acc_scr[...] + gate * _dot(y, wb_ref[...])
    acc_scr[...] = acc
    o_ref[...] = acc.astype(BF16)


def _merge(h, yc, ym, yf, w_gate, w_branch, bm_pref=1024, bn_pref=1024):
    m, d = h.shape
    bw = yc.shape[1]
    bm, bn = _blk(m, bm_pref), _blk(d, bn_pref)
    nc = d // bn
    y_spec = pl.BlockSpec((bm, bw), lambda i, c, b: (i, 0))
    return pl.pallas_call(
        _merge_kernel,
        grid=(m // bm, nc, 3),
        in_specs=[
            pl.BlockSpec((bm, d), lambda i, c, b: (i, 0)),
            y_spec, y_spec, y_spec,
            pl.BlockSpec((d, bn), lambda i, c, b: (0, b * nc + c)),
            pl.BlockSpec((None, bw, bn), lambda i, c, b: (b, 0, c)),
        ],
        out_specs=pl.BlockSpec((bm, bn), lambda i, c, b: (i, c)),
        out_shape=jax.ShapeDtypeStruct((m, d), BF16),
        scratch_shapes=[pltpu.VMEM((bm, bn), F32)],
        name="gated_merge",
        compiler_params=_params("parallel", "parallel", "arbitrary"),
    )(h, yc, ym, yf, w_gate, w_branch)


def kernel(x_prompt, x_sample, cache_k, cache_v, cache_logf, page_table, state_conv, state_C, state_n, state_m, norm_ffn1, ffn1_gate, ffn1_up, ffn1_down, norm_mix, w_in, conv_w, mlstm_b_i, mlstm_b_f, mlstm_norm, fox_b_f, fox_q_norm, fox_k_norm, w_branch, w_out, norm_ffn2, ffn2_gate, ffn2_up, ffn2_down):
    depth = w_in.shape[0]
    n_pr, t_pr, d = x_prompt.shape
    n_dec = x_sample.shape[0]
    conv_c = conv_w.shape[2]
    nh_m, dk, dv = state_C.shape[2:]
    n_pool, page, nh_f, dh = cache_k.shape[1:]
    fox_c = nh_f * dh
    assert x_sample.shape[1] == 1, "sample group is one new token per sequence"
    assert 2 * nh_m + nh_f <= LANES

    o_conv = 0
    o_mq = 3 * conv_c
    o_mo = o_mq + 2 * nh_m * dk + nh_m * dv
    o_mi = o_mo + nh_m * dv
    o_fq = o_mi + 2 * nh_m
    o_ff = o_fq + 3 * fox_c
    o_gz = o_ff + nh_f
    gate_col = 2 * nh_m

    xp = x_prompt.reshape(n_pr * t_pr, d)
    xs = x_sample.reshape(n_dec, d)
    clf = cache_logf.reshape(depth, n_pool, 1, page * nh_f)

    mqkv_scale = jnp.concatenate([jnp.ones((nh_m * dk,), F32),
                                  jnp.full((nh_m * dk,), dk ** -0.5, F32),
                                  jnp.ones((nh_m * dv,), F32)])[None, :]
    outs = {k: [] for k in ("lp", "ks", "vs", "ls", "cbp", "cbs", "cp", "np", "mp", "ns", "ms")}
    kv_stacks = None
    c_stack = None
    for l in range(depth):
        bf = lambda a: a.astype(BF16)
        wi = w_in[l]
        w_conv = bf(wi[:, o_conv:o_mq])
        w_mqkv = bf(wi[:, o_mq:o_mo])
        w_mo = bf(wi[:, o_mo:o_mi])
        w_small = bf(jnp.concatenate(
            [wi[:, o_mi:o_fq], wi[:, o_ff:o_gz],
             jnp.zeros((d, LANES - 2 * nh_m - nh_f), F32)], axis=1))
        w_fox = bf(wi[:, o_fq:o_ff])
        w_gate = bf(wi[:, o_gz:])
        w_br, w_o = bf(w_branch[l]), bf(w_out[l])
        n1, nm, n2 = norm_ffn1[l][None, :], norm_mix[l][None, :], norm_ffn2[l][None, :]
        small_bias = jnp.concatenate(
            [mlstm_b_i[l], mlstm_b_f[l], fox_b_f[l],
             jnp.zeros((LANES - 2 * nh_m - nh_f,), F32)])[None, :]
        m_norm = mlstm_norm[l][None, :]
        qn, kn = fox_q_norm[l][None, :], fox_k_norm[l][None, :]
        cw = conv_w[l]

        xs, hs, *w_ffn1 = _ffn(xs, n1, (ffn1_gate, ffn1_up, ffn1_down), nm, True, layer=l)
        yc_s, u_s = _conv_decode(hs, w_conv, cw, state_conv[l, :, 0, :], state_conv[l, :, 1, :])
        qkv_s = _mm(hs, w_mqkv, BF16, scale=mqkv_scale)
        og_s = _mm(hs, w_mo, BF16, act="sigmoid")
        z_s = _mm(hs, w_small, F32)
        q_s, _, _, kf_s, vf_s = _fox_in(hs, w_fox, qn, kn, dh, 0, 1)
        ym_s, c_stack, n_s, m_s = _mlstm_decode(qkv_s, og_s, z_s, small_bias, m_norm,
                                                state_C, state_n[l], state_m[l], l, c_stack)
        heads = lambda a: a.astype(F32).reshape(n_dec, nh_f, dh)
        yf_s, lf_s = _decode_attn(page_table, heads(q_s), heads(kf_s), heads(vf_s), z_s, small_bias,
                                  cache_k, cache_v, clf, l, gate_col)
        yf_s = yf_s.reshape(n_dec, fox_c).astype(BF16)
        mg_s = _merge(hs, yc_s, ym_s, yf_s, w_gate, w_br)
        xs = _mm(mg_s, w_o, F32, res=xs)
        xs, *w_ffn2 = _ffn(xs, n2, (ffn2_gate, ffn2_up, ffn2_down), n2, False, layer=l)

        xp, hp = _ffn(xp, n1, w_ffn1, nm, True)
        yc_p, tail_p = _conv_prompt(hp, w_conv, cw, n_pr)
        qkv_p = _mm(hp, w_mqkv, BF16, scale=mqkv_scale)
        og_p = _mm(hp, w_mo, BF16, act="sigmoid")
        z_p = _mm(hp, w_small, F32)
        q_p, kb_p, vb_p, *kv_stacks = _fox_in(hp, w_fox, qn, kn, dh, l, depth, kv_stacks)
        ym_p, c_p, n_p, m_p = _mlstm_prompt(qkv_p, og_p, z_p, small_bias, m_norm, n_pr, nh_m, dk, dv)
        lf_p, qa_p, ka_p = _fgate(z_p, small_bias, n_pr, nh_f, gate_col)
        yf_p = _attn(q_p, qa_p, kb_p, ka_p, vb_p, n_pr, nh_f, dh)
        mg_p = _merge(hp, yc_p, ym_p, yf_p, w_gate, w_br)
        xp = _mm(mg_p, w_o, F32, res=xp)
        (xp,) = _ffn(xp, n2, w_ffn2, n2, False, bm_pref=1024)

        outs["lp"].append(lf_p[:, gate_col:gate_col + nh_f].reshape(n_pr, t_pr, nh_f))
        outs["ks"].append(kf_s.reshape(n_dec, 1, nh_f, dh))
        outs["vs"].append(vf_s.reshape(n_dec, 1, nh_f, dh))
        outs["ls"].append(lf_s[:, gate_col:gate_col + nh_f].reshape(n_dec, 1, nh_f))
        outs["cbp"].append(tail_p)
        outs["cbs"].append(jnp.stack([state_conv[l, :, 1, :], u_s], axis=1))
        outs["cp"].append(c_p)
        outs["np"].append(n_p)
        outs["mp"].append(m_p[:, :nh_m, 0])
        outs["ns"].append(n_s)
        outs["ms"].append(m_s)

    st = {k: jnp.stack(v) for k, v in outs.items()}
    k_prompt, v_prompt = (a.reshape(depth, n_pr, t_pr, nh_f, dh) for a in kv_stacks)
    return (xp.reshape(n_pr, t_pr, d), xs.reshape(n_dec, 1, d),
            k_prompt, v_prompt, st["lp"], st["ks"], st["vs"], st["ls"],
            st["cbp"], st["cbs"], st["cp"], st["np"], st["mp"], c_stack, st["ns"], st["ms"])
```
